```python
import math
import jax
import jax.numpy as jnp
from jax import lax
import numpy as np

D_MODEL = 2048
BATCH = 2
SEQ = 4096
DEPTH = 2

N_A_LAYERS = max(1, DEPTH // 2)
N_B_LAYERS = DEPTH - N_A_LAYERS
S5_GROUP_CH = 16
S5_GROUPS = D_MODEL // S5_GROUP_CH
S5_STATE = 64
HEAD_DIM = 128
N_HEADS = D_MODEL // HEAD_DIM
N_KV_HEADS = 4
DILATED_PATTERNS = ((128, 1), (512, 4), (2048, 16))
N_GROUPS = len(DILATED_PATTERNS)
FFN_HIDDEN = ((8 * D_MODEL // 3 + 255) // 256) * 256
EPS = 1e-6
NEG_INF = -1e30

kernel_name = "yoco_s5_dilated_attention_hybrid"


def rms_norm(x, g):
    xf = x.astype(jnp.float32)
    r = lax.rsqrt(jnp.mean(xf * xf, axis=-1, keepdims=True) + EPS)
    return (xf * r * g.astype(jnp.float32)).astype(x.dtype)


def swiglu(h, w_in, w_out):
    gate, up = jnp.split(h @ w_in, 2, axis=-1)
    return (jax.nn.silu(gate) * up) @ w_out


def s5_mixer(u, lam_re, lam_im, log_dt, b_re, b_im, c_re, c_im, d_skip, w_glu):
    bsz, seq, dm = u.shape
    f32 = jnp.float32
    ug = u.astype(f32).reshape(bsz, seq, S5_GROUPS, S5_GROUP_CH)
    lr = lam_re.astype(f32)
    li = lam_im.astype(f32)
    dt = jnp.exp(log_dt.astype(f32))[:, None]
    mag = jnp.exp(lr * dt)
    ang = li * dt
    lb_re = mag * jnp.cos(ang)
    lb_im = mag * jnp.sin(ang)
    nr = lb_re - 1.0
    den = lr * lr + li * li
    f_re = (nr * lr + lb_im * li) / den
    f_im = (lb_im * lr - nr * li) / den
    br = b_re.astype(f32)
    bi = b_im.astype(f32)
    bb_re = f_re[..., None] * br - f_im[..., None] * bi
    bb_im = f_re[..., None] * bi + f_im[..., None] * br
    bu_re = jnp.einsum('bsgc,gpc->bsgp', ug, bb_re)
    bu_im = jnp.einsum('bsgc,gpc->bsgp', ug, bb_im)
    a_re = jnp.broadcast_to(lb_re, bu_re.shape)
    a_im = jnp.broadcast_to(lb_im, bu_im.shape)

    def combine(e1, e2):
        a1r, a1i, b1r, b1i = e1
        a2r, a2i, b2r, b2i = e2
        return (a2r * a1r - a2i * a1i,
                a2r * a1i + a2i * a1r,
                a2r * b1r - a2i * b1i + b2r,
                a2r * b1i + a2i * b1r + b2i)

    _, _, xs_re, xs_im = lax.associative_scan(combine, (a_re, a_im, bu_re, bu_im), axis=1)
    y = (jnp.einsum('bsgp,gcp->bsgc', xs_re, c_re.astype(f32))
         - jnp.einsum('bsgp,gcp->bsgc', xs_im, c_im.astype(f32)))
    y = y.reshape(bsz, seq, dm) + d_skip.astype(f32) * u.astype(f32)
    z = jax.nn.gelu(y).astype(u.dtype)
    val, gate = jnp.split(z @ w_glu, 2, axis=-1)
    return val * jax.nn.sigmoid(gate)


def dilated_window_attention(q, k, v, window, dilation):
    bsz, seq, nh, hd = q.shape
    nkv = k.shape[2]
    rep = nh // nkv
    n = seq // dilation
    blk = window // dilation
    nb = -(-n // blk)
    pad = nb * blk - n

    def to_residue(t):
        h = t.shape[2]
        t = t.reshape(bsz, n, dilation, h, hd).transpose(0, 2, 1, 3, 4)
        t = t.reshape(bsz * dilation, n, h, hd)
        t = jnp.pad(t, ((0, 0), (0, pad), (0, 0), (0, 0)))
        return t.reshape(bsz * dilation, nb, blk, h, hd)

    def with_prev(t):
        prev = jnp.pad(t, ((0, 0), (1, 0), (0, 0), (0, 0), (0, 0)))[:, :-1]
        return jnp.concatenate([prev, t], axis=2)

    qr = to_residue(q).reshape(bsz * dilation, nb, blk, nkv, rep, hd)
    kb = with_prev(to_residue(k))
    vb = with_prev(to_residue(v))
    scores = jnp.einsum('znqkge,znske->znkgqs', qr, kb).astype(jnp.float32) * (hd ** -0.5)
    qi = jnp.arange(blk)[:, None]
    si = jnp.arange(2 * blk)[None, :]
    dist = qi + blk - si
    band = (dist >= 0) & (dist <= blk)
    valid = band[None] & ((jnp.arange(nb)[:, None, None] > 0) | (si[None] >= blk))
    scores = jnp.where(valid[None, :, None, None], scores, NEG_INF)
    m = jnp.max(scores, axis=-1, keepdims=True)
    p = jnp.exp(scores - m)
    l = jnp.sum(p, axis=-1, keepdims=True)
    out = jnp.einsum('znkgqs,znske->znqkge', (p / l).astype(v.dtype), vb)
    lse = (m + jnp.log(l))[..., 0]
    out = out.reshape(bsz * dilation, nb * blk, nh, hd)[:, :n]
    out = out.reshape(bsz, dilation, n, nh, hd).transpose(0, 2, 1, 3, 4).reshape(bsz, seq, nh, hd)
    lse = lse.transpose(0, 1, 4, 2, 3).reshape(bsz * dilation, nb * blk, nh)[:, :n]
    lse = lse.reshape(bsz, dilation, n, nh).transpose(0, 2, 1, 3).reshape(bsz, seq, nh)
    return out, lse


def dilated_mixer(h, k, v, w_q, w_o):
    bsz, seq, _ = h.shape
    q = (h @ w_q).reshape(bsz, seq, N_GROUPS, N_HEADS, HEAD_DIM)
    outs = []
    lses = []
    for g, (window, dilation) in enumerate(DILATED_PATTERNS):
        o, l = dilated_window_attention(q[:, :, g], k, v, window, dilation)
        outs.append(o)
        lses.append(l)
    wts = jax.nn.softmax(jnp.stack(lses, axis=0), axis=0)
    o = jnp.sum(wts[..., None] * jnp.stack(outs, axis=0).astype(jnp.float32), axis=0)
    return o.astype(h.dtype).reshape(bsz, seq, N_HEADS * HEAD_DIM) @ w_o


def setup_inputs(seed: int = 0) -> dict:
    key = jax.random.key(seed)
    ks = jax.random.split(key, 24)
    f32 = jnp.float32

    def nrm(k, shape, scale):
        return jax.random.normal(k, shape, f32) * scale

    na, nbl, D, F = N_A_LAYERS, N_B_LAYERS, D_MODEL, FFN_HIDDEN
    G, P, C = S5_GROUPS, S5_STATE, S5_GROUP_CH
    x = nrm(ks[0], (BATCH, SEQ, D), 1.0)
    n_idx = jnp.arange(P, dtype=f32)
    s5_lam_re = -0.5 + nrm(ks[1], (na, G, P), 0.01)
    s5_lam_im = math.pi * n_idx + nrm(ks[2], (na, G, P), 0.01)
    s5_log_dt = jax.random.uniform(ks[3], (na, G), f32, math.log(1e-3), math.log(1e-1))
    s5_b_re = nrm(ks[4], (na, G, P, C), (2 * C) ** -0.5)
    s5_b_im = nrm(ks[5], (na, G, P, C), (2 * C) ** -0.5)
    s5_c_re = nrm(ks[6], (na, G, C, P), (2 * P) ** -0.5)
    s5_c_im = nrm(ks[7], (na, G, C, P), (2 * P) ** -0.5)
    s5_d = nrm(ks[8], (na, D), 1.0)
    s5_w_glu = nrm(ks[9], (na, D, 2 * D), D ** -0.5)
    a_norm_mix = 1.0 + nrm(ks[10], (na, D), 0.02)
    ffn_norm = 1.0 + nrm(ks[11], (DEPTH, D), 0.02)
    ffn_w_in = nrm(ks[12], (DEPTH, D, 2 * F), D ** -0.5)
    ffn_w_out = nrm(ks[13], (DEPTH, F, D), F ** -0.5)
    b_norm_mix = 1.0 + nrm(ks[14], (nbl, D), 0.02)
    attn_w_q = nrm(ks[15], (nbl, D, N_GROUPS * N_HEADS * HEAD_DIM), D ** -0.5)
    attn_w_o = nrm(ks[16], (nbl, N_HEADS * HEAD_DIM, D), (N_HEADS * HEAD_DIM) ** -0.5)
    kv_norm = 1.0 + nrm(ks[17], (D,), 0.02)
    w_kv = nrm(ks[18], (D, 2 * N_KV_HEADS * HEAD_DIM), D ** -0.5)
    final_norm = 1.0 + nrm(ks[19], (D,), 0.02)
    return {"x": x, "s5_lam_re": s5_lam_re, "s5_lam_im": s5_lam_im, "s5_log_dt": s5_log_dt,
            "s5_b_re": s5_b_re, "s5_b_im": s5_b_im, "s5_c_re": s5_c_re, "s5_c_im": s5_c_im,
            "s5_d": s5_d, "s5_w_glu": s5_w_glu, "a_norm_mix": a_norm_mix,
            "ffn_norm": ffn_norm, "ffn_w_in": ffn_w_in, "ffn_w_out": ffn_w_out,
            "b_norm_mix": b_norm_mix, "attn_w_q": attn_w_q, "attn_w_o": attn_w_o,
            "kv_norm": kv_norm, "w_kv": w_kv, "final_norm": final_norm}


def reference(x, s5_lam_re, s5_lam_im, s5_log_dt, s5_b_re, s5_b_im, s5_c_re, s5_c_im,
              s5_d, s5_w_glu, a_norm_mix, ffn_norm, ffn_w_in, ffn_w_out,
              b_norm_mix, attn_w_q, attn_w_o, kv_norm, w_kv, final_norm):
    bsz, seq, _ = x.shape
    k = None
    v = None
    for layer in range(DEPTH):
        if layer < N_A_LAYERS:
            i = layer
            h = rms_norm(x, a_norm_mix[i])
            x = x + s5_mixer(h, s5_lam_re[i], s5_lam_im[i], s5_log_dt[i], s5_b_re[i], s5_b_im[i],
                             s5_c_re[i], s5_c_im[i], s5_d[i], s5_w_glu[i])
        else:
            j = layer - N_A_LAYERS
            if j == 0:
                kv = (rms_norm(x, kv_norm) @ w_kv).reshape(bsz, seq, 2, N_KV_HEADS, HEAD_DIM)
                k = kv[:, :, 0]
                v = kv[:, :, 1]
            h = rms_norm(x, b_norm_mix[j])
            x = x + dilated_mixer(h, k, v, attn_w_q[j], attn_w_o[j])
        x = x + swiglu(rms_norm(x, ffn_norm[layer]), ffn_w_in[layer], ffn_w_out[layer])
    return rms_norm(x, final_norm)
```

```python
import functools
import math

import jax
import jax.numpy as jnp
from jax import lax
from jax.experimental import pallas as pl
from jax.experimental.pallas import tpu as pltpu

_F32 = jnp.float32
_BF16 = jnp.bfloat16
_EPS = 1e-6
_NEG_INF = -1e30

_LANES = 128
_SUBLANES = 8
_VMEM_LIMIT_BYTES = 56 * 1024 * 1024

_S5_GROUP_CH = 16
_S5_STATE = 64
_HEAD_DIM = 128
_N_KV_HEADS = 4
_DILATED_PATTERNS = ((128, 1), (512, 4), (2048, 16))
_GROUPS_PER_TILE = _LANES // _S5_GROUP_CH
_HALF = _GROUPS_PER_TILE * _S5_STATE
_STATE_W = 2 * _HALF


def _params(sem):
    return pltpu.CompilerParams(dimension_semantics=sem, vmem_limit_bytes=_VMEM_LIMIT_BYTES)


def _rms_scale(x):
    return lax.rsqrt(jnp.mean(x * x, axis=-1, keepdims=True) + _EPS)


def _sigmoid(v):
    return 1.0 / (1.0 + jnp.exp(-v))


def _cmul(ar, ai, br, bi):
    return ar * br - ai * bi, ar * bi + ai * br


def _s5_kernel(x_ref, g_ref, dsk_ref, lam_ref, wb_ref, wc_ref, p_ref, pt_ref, z_ref,
               h_ref, hbp_ref, y_ref, bu_ref, xb_ref, carry_ref, cp_ref, lvl_ref, *, lc):
    t = pl.program_id(1)
    n_tiles = lam_ref.shape[0]
    rows = lax.broadcasted_iota(jnp.int32, (_SUBLANES, _HALF), 0)

    @pl.when(t == 0)
    def _init():
        carry_ref[...] = jnp.zeros_like(carry_ref)

        def init_m(m, c):
            pr = lam_ref[m, :, 0:_HALF]
            pi = lam_ref[m, :, _HALF:_STATE_W]
            for _ in range(int(math.log2(lc))):
                pr, pi = _cmul(pr, pi, pr, pi)
            qr, qi = pr, pi
            for lv in range(3):
                lvl_ref[m, lv * _SUBLANES:(lv + 1) * _SUBLANES, 0:_HALF] = qr
                lvl_ref[m, lv * _SUBLANES:(lv + 1) * _SUBLANES, _HALF:_STATE_W] = qi
                qr, qi = _cmul(qr, qi, qr, qi)
            cr, ci = pr, pi
            outr, outi = pr, pi
            for c_idx in range(1, _SUBLANES):
                cr, ci = _cmul(cr, ci, pr, pi)
                outr = jnp.where(rows == c_idx, cr, outr)
                outi = jnp.where(rows == c_idx, ci, outi)
            cp_ref[m, :, 0:_HALF] = outr
            cp_ref[m, :, _HALF:_STATE_W] = outi
            return c

        lax.fori_loop(0, n_tiles, init_m, 0)

    x = x_ref[...]
    h = x * _rms_scale(x) * g_ref[...]
    h_ref[...] = h
    hp = jnp.dot(p_ref[...], h.astype(_BF16), preferred_element_type=_F32)
    for m in range(n_tiles):
        hbp_ref[m] = hp[:, m * _LANES:(m + 1) * _LANES].astype(_BF16)

    def m_body(m, c):
        bu_ref[...] = jnp.dot(hbp_ref[m], wb_ref[m], preferred_element_type=_F32)
        lr = lam_ref[m, :, 0:_HALF]
        li = lam_ref[m, :, _HALF:_STATE_W]

        xr = jnp.zeros((_SUBLANES, _HALF), _F32)
        xi = jnp.zeros((_SUBLANES, _HALF), _F32)
        for tau in range(lc):
            r0 = tau * _SUBLANES
            tr, ti = _cmul(lr, li, xr, xi)
            xr = tr + bu_ref[r0:r0 + _SUBLANES, 0:_HALF]
            xi = ti + bu_ref[r0:r0 + _SUBLANES, _HALF:_STATE_W]
            bu_ref[r0:r0 + _SUBLANES, 0:_HALF] = xr
            bu_ref[r0:r0 + _SUBLANES, _HALF:_STATE_W] = xi

        zr, zi = xr, xi
        for lv, s in enumerate((1, 2, 4)):
            ar = lvl_ref[m, lv * _SUBLANES:(lv + 1) * _SUBLANES, 0:_HALF]
            ai = lvl_ref[m, lv * _SUBLANES:(lv + 1) * _SUBLANES, _HALF:_STATE_W]
            sr = jnp.where(rows >= s, pltpu.roll(zr, s, 0), 0.0)
            si = jnp.where(rows >= s, pltpu.roll(zi, s, 0), 0.0)
            tr, ti = _cmul(ar, ai, sr, si)
            zr = zr + tr
            zi = zi + ti
        pr = carry_ref[m, :, 0:_HALF]
        pi = carry_ref[m, :, _HALF:_STATE_W]
        tr, ti = _cmul(cp_ref[m, :, 0:_HALF], cp_ref[m, :, _HALF:_STATE_W], pr, pi)
        er = zr + tr
        ei = zi + ti
        cr = jnp.where(rows >= 1, pltpu.roll(er, 1, 0), pr)
        ci = jnp.where(rows >= 1, pltpu.roll(ei, 1, 0), pi)
        carry_ref[m, :, 0:_HALF] = jnp.broadcast_to(er[_SUBLANES - 1:_SUBLANES, :], (_SUBLANES, _HALF))
        carry_ref[m, :, _HALF:_STATE_W] = jnp.broadcast_to(ei[_SUBLANES - 1:_SUBLANES, :], (_SUBLANES, _HALF))

        for tau in range(0, lc, 2):
            r0 = tau * _SUBLANES
            cr, ci = _cmul(lr, li, cr, ci)
            x0r = bu_ref[r0:r0 + _SUBLANES, 0:_HALF] + cr
            x0i = bu_ref[r0:r0 + _SUBLANES, _HALF:_STATE_W] + ci
            cr, ci = _cmul(lr, li, cr, ci)
            x1r = bu_ref[r0 + _SUBLANES:r0 + 2 * _SUBLANES, 0:_HALF] + cr
            x1i = bu_ref[r0 + _SUBLANES:r0 + 2 * _SUBLANES, _HALF:_STATE_W] + ci
            xb_ref[r0:r0 + 2 * _SUBLANES, 0:_HALF] = jnp.concatenate([x0r, x1r], axis=0).astype(_BF16)
            xb_ref[r0:r0 + 2 * _SUBLANES, _HALF:_STATE_W] = jnp.concatenate([x0i, x1i], axis=0).astype(_BF16)

        y_ref[m] = jnp.dot(xb_ref[...], wc_ref[m], preferred_element_type=_F32)
        return c

    lax.fori_loop(0, n_tiles, m_body, 0)

    y = jnp.concatenate([y_ref[m] for m in range(n_tiles)], axis=1)
    y_hi = y.astype(_BF16)
    y_lo = (y - y_hi.astype(_F32)).astype(_BF16)
    pt = pt_ref[...]
    yn = (jnp.dot(pt, y_hi, preferred_element_type=_F32)
          + jnp.dot(pt, y_lo, preferred_element_type=_F32))
    yn = yn + dsk_ref[...] * h_ref[...]
    cdf = 0.5 * (1.0 + jnp.tanh(math.sqrt(2.0 / math.pi) * (yn + 0.044715 * (yn * yn * yn))))
    z_ref[...] = (yn * cdf).astype(z_ref.dtype)


def _s5_tables(lam_re, lam_im, log_dt, b_re, b_im, c_re, c_im):
    g, p = lam_re.shape
    c = b_re.shape[-1]
    nm = g // _GROUPS_PER_TILE
    lr = lam_re.astype(_F32)
    li = lam_im.astype(_F32)
    dt = jnp.exp(log_dt.astype(_F32))[:, None]
    mag = jnp.exp(lr * dt)
    ang = li * dt
    lb_re = mag * jnp.cos(ang)
    lb_im = mag * jnp.sin(ang)
    nr = lb_re - 1.0
    den = lr * lr + li * li
    f_re = (nr * lr + lb_im * li) / den
    f_im = (lb_im * lr - nr * li) / den
    br = b_re.astype(_F32)
    bi = b_im.astype(_F32)
    bb_re = f_re[..., None] * br - f_im[..., None] * bi
    bb_im = f_re[..., None] * bi + f_im[..., None] * br
    eye = jnp.eye(_GROUPS_PER_TILE, dtype=_F32)

    def blk_b(bb):
        t = bb.reshape(nm, _GROUPS_PER_TILE, p, c)
        return jnp.einsum('mgpc,gh->mgchp', t, eye).reshape(nm, _GROUPS_PER_TILE * c, _GROUPS_PER_TILE * p)

    def blk_c(cc):
        t = cc.reshape(nm, _GROUPS_PER_TILE, c, p)
        return jnp.einsum('mgcp,gh->mgphc', t, eye).reshape(nm, _GROUPS_PER_TILE * p, _GROUPS_PER_TILE * c)

    wb = jnp.concatenate([blk_b(bb_re), blk_b(bb_im)], axis=-1).astype(_BF16)
    wc = jnp.concatenate([blk_c(c_re.astype(_F32)), blk_c(-c_im.astype(_F32))], axis=1).astype(_BF16)
    lam = jnp.concatenate([lb_re.reshape(nm, _HALF), lb_im.reshape(nm, _HALF)], axis=-1)
    lam8 = jnp.broadcast_to(lam[:, None, :], (nm, _SUBLANES, _STATE_W))
    return lam8, wb, wc


def _s5_mixer(x2, bsz, seq, gain, d_skip, lam8, wb, wc, *, tile=256):
    t_tok, d = x2.shape
    nm = lam8.shape[0]
    lc = tile // _SUBLANES
    nt = seq // tile
    r = jnp.arange(tile)
    col = (r % _SUBLANES) * lc + r // _SUBLANES
    perm = (col[:, None] == jnp.arange(tile)[None, :]).astype(_BF16)
    perm_t = perm.T
    const3 = lambda b, t: (0, 0, 0)
    const2 = lambda b, t: (0, 0)
    return pl.pallas_call(
        functools.partial(_s5_kernel, lc=lc),
        grid=(bsz, nt),
        in_specs=[
            pl.BlockSpec((tile, d), lambda b, t: (b * nt + t, 0)),
            pl.BlockSpec((1, d), const2),
            pl.BlockSpec((1, d), const2),
            pl.BlockSpec((nm, _SUBLANES, _STATE_W), const3),
            pl.BlockSpec((nm, _LANES, _STATE_W), const3),
            pl.BlockSpec((nm, _STATE_W, _LANES), const3),
            pl.BlockSpec((tile, tile), const2),
            pl.BlockSpec((tile, tile), const2),
        ],
        out_specs=pl.BlockSpec((tile, d), lambda b, t: (b * nt + t, 0)),
        out_shape=jax.ShapeDtypeStruct((t_tok, d), _BF16),
        scratch_shapes=[
            pltpu.VMEM((tile, d), _F32),
            pltpu.VMEM((nm, tile, _LANES), _BF16),
            pltpu.VMEM((nm, tile, _LANES), _F32),
            pltpu.VMEM((tile, _STATE_W), _F32),
            pltpu.VMEM((tile, _STATE_W), _BF16),
            pltpu.VMEM((nm, _SUBLANES, _STATE_W), _F32),
            pltpu.VMEM((nm, _SUBLANES, _STATE_W), _F32),
            pltpu.VMEM((nm, 3 * _SUBLANES, _STATE_W), _F32),
        ],
        compiler_params=_params(("arbitrary", "arbitrary")),
        name="s5_mixer",
    )(x2, gain.reshape(1, d).astype(_F32), d_skip.reshape(1, d).astype(_F32), lam8, wb, wc, perm, perm_t)


def _glu_kernel(z_ref, wv_ref, wg_ref, x_ref, o_ref):
    z = z_ref[...]
    val = jnp.dot(z, wv_ref[...], preferred_element_type=_F32)
    gate = jnp.dot(z, wg_ref[...], preferred_element_type=_F32)
    o_ref[...] = x_ref[...] + val * _sigmoid(gate)


def _glu_residual(z, w_glu, x2, *, tm=1024, tn=512):
    t_tok, d = x2.shape
    nj = d // tn
    return pl.pallas_call(
        _glu_kernel,
        grid=(t_tok // tm, nj),
        in_specs=[
            pl.BlockSpec((tm, d), lambda i, j: (i, 0)),
            pl.BlockSpec((d, tn), lambda i, j: (0, j)),
            pl.BlockSpec((d, tn), lambda i, j: (0, j + nj)),
            pl.BlockSpec((tm, tn), lambda i, j: (i, j)),
        ],
        out_specs=pl.BlockSpec((tm, tn), lambda i, j: (i, j)),
        out_shape=jax.ShapeDtypeStruct((t_tok, d), _F32),
        compiler_params=_params(("parallel", "arbitrary")),
        name="glu_residual",
    )(z, w_glu, w_glu, x2)


def _ffn_kernel(x_ref, g_ref, wg_ref, wu_ref, wo_ref, fg_ref, o_ref, hn_ref, *, final_norm):
    f = pl.program_id(1)

    @pl.when(f == 0)
    def _start():
        x = x_ref[...]
        hn_ref[...] = (x * _rms_scale(x) * g_ref[...]).astype(_BF16)
        o_ref[...] = x

    hn = hn_ref[...]
    a = jnp.dot(hn, wg_ref[...], preferred_element_type=_F32)
    u = jnp.dot(hn, wu_ref[...], preferred_element_type=_F32)
    act = (a * _sigmoid(a) * u).astype(_BF16)
    o_ref[...] += jnp.dot(act, wo_ref[...], preferred_element_type=_F32)

    if final_norm:
        @pl.when(f == pl.num_programs(1) - 1)
        def _finish():
            y = o_ref[...]
            o_ref[...] = y * _rms_scale(y) * fg_ref[...]


def _ffn_residual(x2, gain, w_in, w_out, final_gain=None, *, tm=512, tf=512):
    t_tok, d = x2.shape
    hidden = w_out.shape[0]
    nf = hidden // tf
    final_norm = final_gain is not None
    fg = (final_gain if final_norm else gain).reshape(1, d).astype(_F32)
    return pl.pallas_call(
        functools.partial(_ffn_kernel, final_norm=final_norm),
        grid=(t_tok // tm, nf),
        in_specs=[
            pl.BlockSpec((tm, d), lambda i, f: (i, 0)),
            pl.BlockSpec((1, d), lambda i, f: (0, 0)),
            pl.BlockSpec((d, tf), lambda i, f: (0, f)),
            pl.BlockSpec((d, tf), lambda i, f: (0, f + nf)),
            pl.BlockSpec((tf, d), lambda i, f: (f, 0)),
            pl.BlockSpec((1, d), lambda i, f: (0, 0)),
        ],
        out_specs=pl.BlockSpec((tm, d), lambda i, f: (i, 0)),
        out_shape=jax.ShapeDtypeStruct((t_tok, d), _F32),
        scratch_shapes=[pltpu.VMEM((tm, d), _BF16)],
        compiler_params=_params(("parallel", "arbitrary")),
        name="ffn_residual",
    )(x2, gain.reshape(1, d).astype(_F32), w_in, w_in, w_out, fg)


def _nmm_kernel(x_ref, g_ref, w_ref, o_ref, hn_ref):
    @pl.when(pl.program_id(1) == 0)
    def _start():
        x = x_ref[...]
        hn_ref[...] = (x * _rms_scale(x) * g_ref[...]).astype(_BF16)

    o_ref[...] = jnp.dot(hn_ref[...], w_ref[...], preferred_element_type=_F32).astype(o_ref.dtype)


def _norm_project(x2, gain, w, *, tm=1024, tn=1024):
    t_tok, d = x2.shape
    n = w.shape[1]
    tn = min(tn, n)
    return pl.pallas_call(
        _nmm_kernel,
        grid=(t_tok // tm, n // tn),
        in_specs=[
            pl.BlockSpec((tm, d), lambda i, j: (i, 0)),
            pl.BlockSpec((1, d), lambda i, j: (0, 0)),
            pl.BlockSpec((d, tn), lambda i, j: (0, j)),
        ],
        out_specs=pl.BlockSpec((tm, tn), lambda i, j: (i, j)),
        out_shape=jax.ShapeDtypeStruct((t_tok, n), _BF16),
        scratch_shapes=[pltpu.VMEM((tm, d), _BF16)],
        compiler_params=_params(("parallel", "arbitrary")),
        name="norm_project",
    )(x2, gain.reshape(1, d).astype(_F32), w)


def _attn_kernel(q_ref, kp_ref, kc_ref, vp_ref, vc_ref, o_ref, st_ref, *, blk, rep, scale):
    i = pl.program_id(2)
    qi = lax.broadcasted_iota(jnp.int32, (blk, 2 * blk), 0)
    si = lax.broadcasted_iota(jnp.int32, (blk, 2 * blk), 1)
    dist = qi + blk - si
    valid = (dist >= 0) & (dist <= blk) & ((i > 0) | (si >= blk))
    valid = jnp.concatenate([valid] * rep, axis=0)
    lane = lax.broadcasted_iota(jnp.int32, (blk, _LANES), 1)
    st = jnp.zeros((blk, _LANES), _F32)
    n_kv = kc_ref.shape[1] // _HEAD_DIM
    for kvh in range(n_kv):
        ks = slice(kvh * _HEAD_DIM, (kvh + 1) * _HEAD_DIM)
        k = jnp.concatenate([kp_ref[:, ks], kc_ref[:, ks]], axis=0)
        v = jnp.concatenate([vp_ref[:, ks], vc_ref[:, ks]], axis=0)
        q = jnp.concatenate(
            [q_ref[:, (kvh * rep + rp) * _HEAD_DIM:(kvh * rep + rp + 1) * _HEAD_DIM] for rp in range(rep)],
            axis=0)
        s = lax.dot_general(q, k, (((1,), (1,)), ((), ())), preferred_element_type=_F32) * scale
        s = jnp.where(valid, s, _NEG_INF)
        m = jnp.max(s, axis=-1, keepdims=True)
        p = jnp.exp(s - m)
        l = jnp.sum(p, axis=-1, keepdims=True)
        pn = (p * (1.0 / l)).astype(_BF16)
        o = jnp.dot(pn, v, preferred_element_type=_F32)
        lse = m + jnp.log(l)
        for rp in range(rep):
            hd = kvh * rep + rp
            o_ref[:, hd * _HEAD_DIM:(hd + 1) * _HEAD_DIM] = o[rp * blk:(rp + 1) * blk].astype(o_ref.dtype)
            st = jnp.where(lane == hd, lse[rp * blk:(rp + 1) * blk], st)
    st_ref[...] = st


def _dilated_attention(q, kv, bsz, seq, group, n_groups, window, dilation):
    t_tok = q.shape[0]
    qw = q.shape[1] // n_groups
    kw = kv.shape[1] // 2
    n_heads = qw // _HEAD_DIM
    rep = n_heads // (kw // _HEAD_DIM)
    blk = window // dilation
    n = seq // dilation
    assert n % blk == 0 and blk == _LANES and n_heads <= _LANES
    nb = n // blk
    q3 = q.reshape(bsz, n, dilation * n_groups * qw)
    kv3 = kv.reshape(bsz, n, dilation * 2 * kw)
    prev = lambda i: jnp.maximum(i - 1, 0)
    o3, st3 = pl.pallas_call(
        functools.partial(_attn_kernel, blk=blk, rep=rep, scale=_HEAD_DIM ** -0.5),
        grid=(bsz, dilation, nb),
        in_specs=[
            pl.BlockSpec((None, blk, qw), lambda b, r, i: (b, i, r * n_groups + group)),
            pl.BlockSpec((None, blk, kw), lambda b, r, i: (b, prev(i), 2 * r)),
            pl.BlockSpec((None, blk, kw), lambda b, r, i: (b, i, 2 * r)),
            pl.BlockSpec((None, blk, kw), lambda b, r, i: (b, prev(i), 2 * r + 1)),
            pl.BlockSpec((None, blk, kw), lambda b, r, i: (b, i, 2 * r + 1)),
        ],
        out_specs=[
            pl.BlockSpec((None, blk, qw), lambda b, r, i: (b, i, r)),
            pl.BlockSpec((None, blk, _LANES), lambda b, r, i: (b, i, r)),
        ],
        out_shape=[
            jax.ShapeDtypeStruct((bsz, n, dilation * qw), _BF16),
            jax.ShapeDtypeStruct((bsz, n, dilation * _LANES), _F32),
        ],
        compiler_params=_params(("parallel", "parallel", "arbitrary")),
        name=f"dilated_attention_d{dilation}",
    )(q3, kv3, kv3, kv3, kv3)
    return o3.reshape(t_tok, qw), st3.reshape(t_tok, _LANES)


def _combine_kernel(*refs, n_groups, n_heads):
    o_refs = refs[:n_groups]
    s_refs = refs[n_groups:2 * n_groups]
    x_ref, wo_ref, out_ref, comb_ref = refs[2 * n_groups:]
    lses = [s[...] for s in s_refs]
    mx = lses[0]
    for l in lses[1:]:
        mx = jnp.maximum(mx, l)
    es = [jnp.exp(l - mx) for l in lses]
    den = es[0]
    for e in es[1:]:
        den = den + e
    inv = 1.0 / den
    wts = [e * inv for e in es]
    for hd in range(n_heads):
        cs = slice(hd * _HEAD_DIM, (hd + 1) * _HEAD_DIM)
        acc = wts[0][:, hd:hd + 1] * o_refs[0][:, cs].astype(_F32)
        for g in range(1, n_groups):
            acc = acc + wts[g][:, hd:hd + 1] * o_refs[g][:, cs].astype(_F32)
        comb_ref[:, cs] = acc.astype(_BF16)
    out_ref[...] = x_ref[...] + jnp.dot(comb_ref[...], wo_ref[...], preferred_element_type=_F32)


def _combine_project(outs, stats, x2, w_o, *, tm=512):
    t_tok, d = x2.shape
    qw = outs[0].shape[1]
    n_groups = len(outs)
    row = lambda i: (i, 0)
    return pl.pallas_call(
        functools.partial(_combine_kernel, n_groups=n_groups, n_heads=qw // _HEAD_DIM),
        grid=(t_tok // tm,),
        in_specs=([pl.BlockSpec((tm, qw), row)] * n_groups
                  + [pl.BlockSpec((tm, _LANES), row)] * n_groups
                  + [pl.BlockSpec((tm, d), row), pl.BlockSpec((qw, d), lambda i: (0, 0))]),
        out_specs=pl.BlockSpec((tm, d), row),
        out_shape=jax.ShapeDtypeStruct((t_tok, d), _F32),
        scratch_shapes=[pltpu.VMEM((tm, qw), _BF16)],
        compiler_params=_params(("parallel",)),
        name="combine_project",
    )(*outs, *stats, x2, w_o)


def kernel(x, s5_lam_re, s5_lam_im, s5_log_dt, s5_b_re, s5_b_im, s5_c_re, s5_c_im, s5_d, s5_w_glu, a_norm_mix, ffn_norm, ffn_w_in, ffn_w_out, b_norm_mix, attn_w_q, attn_w_o, kv_norm, w_kv, final_norm):
    bsz, seq, d = x.shape
    n_a = a_norm_mix.shape[0]
    n_b = b_norm_mix.shape[0]
    depth = n_a + n_b
    n_groups = len(_DILATED_PATTERNS)
    x2 = x.reshape(bsz * seq, d).astype(_F32)
    kv = None
    for layer in range(depth):
        if layer < n_a:
            i = layer
            lam8, wb, wc = _s5_tables(s5_lam_re[i], s5_lam_im[i], s5_log_dt[i], s5_b_re[i], s5_b_im[i],
                                      s5_c_re[i], s5_c_im[i])
            z = _s5_mixer(x2, bsz, seq, a_norm_mix[i], s5_d[i], lam8, wb, wc)
            x2 = _glu_residual(z, s5_w_glu[i].astype(_BF16), x2)
        else:
            j = layer - n_a
            if j == 0:
                kv = _norm_project(x2, kv_norm, w_kv.astype(_BF16))
            q = _norm_project(x2, b_norm_mix[j], attn_w_q[j].astype(_BF16))
            outs, stats = [], []
            for g, (window, dilation) in enumerate(_DILATED_PATTERNS):
                o, st = _dilated_attention(q, kv, bsz, seq, g, n_groups, window, dilation)
                outs.append(o)
                stats.append(st)
            x2 = _combine_project(outs, stats, x2, attn_w_o[j].astype(_BF16))
        final_gain = final_norm if layer == depth - 1 else None
        x2 = _ffn_residual(x2, ffn_norm[layer], ffn_w_in[layer].astype(_BF16), ffn_w_out[layer].astype(_BF16),
                           final_gain)
    return x2.reshape(bsz, seq, d).astype(x.dtype)
```

```python
import functools
import math

import jax
import jax.numpy as jnp
from jax import lax
from jax.experimental import pallas as pl
from jax.experimental.pallas import tpu as pltpu

_F32 = jnp.float32
_BF16 = jnp.bfloat16
_EPS = 1e-6
_NEG_INF = -1e30

_LANES = 128
_SUBLANES = 8
_VMEM_LIMIT_BYTES = 56 * 1024 * 1024

_S5_GROUP_CH = 16
_S5_STATE = 64
_HEAD_DIM = 128
_N_KV_HEADS = 4
_DILATED_PATTERNS = ((128, 1), (512, 4), (2048, 16))
_GROUPS_PER_TILE = _LANES // _S5_GROUP_CH
_HALF = _GROUPS_PER_TILE * _S5_STATE
_STATE_W = 2 * _HALF


def _params(sem):
    return pltpu.CompilerParams(dimension_semantics=sem, vmem_limit_bytes=_VMEM_LIMIT_BYTES)


def _rms_scale(x):
    return lax.rsqrt(jnp.mean(x * x, axis=-1, keepdims=True) + _EPS)


def _sigmoid(v):
    return 1.0 / (1.0 + jnp.exp(-v))


def _cmul(ar, ai, br, bi):
    return ar * br - ai * bi, ar * bi + ai * br


def _s5_kernel(x_ref, g_ref, dsk_ref, lam_ref, wb_ref, wc_ref, p_ref, pt_ref, z_ref,
               h_ref, hbp_ref, y_ref, bu_ref, xb_ref, carry_ref, cp_ref, lvl_ref, *, lc):
    t = pl.program_id(1)
    n_tiles = lam_ref.shape[0]
    rows = lax.broadcasted_iota(jnp.int32, (_SUBLANES, _HALF), 0)

    @pl.when(t == 0)
    def _init():
        carry_ref[...] = jnp.zeros_like(carry_ref)

        def init_m(m, c):
            pr = lam_ref[m, :, 0:_HALF]
            pi = lam_ref[m, :, _HALF:_STATE_W]
            for _ in range(int(math.log2(lc))):
                pr, pi = _cmul(pr, pi, pr, pi)
            qr, qi = pr, pi
            for lv in range(3):
                lvl_ref[m, lv * _SUBLANES:(lv + 1) * _SUBLANES, 0:_HALF] = qr
                lvl_ref[m, lv * _SUBLANES:(lv + 1) * _SUBLANES, _HALF:_STATE_W] = qi
                qr, qi = _cmul(qr, qi, qr, qi)
            cr, ci = pr, pi
            outr, outi = pr, pi
            for c_idx in range(1, _SUBLANES):
                cr, ci = _cmul(cr, ci, pr, pi)
                outr = jnp.where(rows == c_idx, cr, outr)
                outi = jnp.where(rows == c_idx, ci, outi)
            cp_ref[m, :, 0:_HALF] = outr
            cp_ref[m, :, _HALF:_STATE_W] = outi
            return c

        lax.fori_loop(0, n_tiles, init_m, 0)

    x = x_ref[...]
    h = x * _rms_scale(x) * g_ref[...]
    h_ref[...] = h
    hp = jnp.dot(p_ref[...], h.astype(_BF16), preferred_element_type=_F32)
    for m in range(n_tiles):
        hbp_ref[m] = hp[:, m * _LANES:(m + 1) * _LANES].astype(_BF16)

    def m_body(m, c):
        bu_ref[...] = jnp.dot(hbp_ref[m], wb_ref[m], preferred_element_type=_F32)
        lr = lam_ref[m, :, 0:_HALF]
        li = lam_ref[m, :, _HALF:_STATE_W]

        xr = jnp.zeros((_SUBLANES, _HALF), _F32)
        xi = jnp.zeros((_SUBLANES, _HALF), _F32)
        for tau in range(lc):
            r0 = tau * _SUBLANES
            tr, ti = _cmul(lr, li, xr, xi)
            xr = tr + bu_ref[r0:r0 + _SUBLANES, 0:_HALF]
            xi = ti + bu_ref[r0:r0 + _SUBLANES, _HALF:_STATE_W]
            bu_ref[r0:r0 + _SUBLANES, 0:_HALF] = xr
            bu_ref[r0:r0 + _SUBLANES, _HALF:_STATE_W] = xi

        zr, zi = xr, xi
        for lv, s in enumerate((1, 2, 4)):
            ar = lvl_ref[m, lv * _SUBLANES:(lv + 1) * _SUBLANES, 0:_HALF]
            ai = lvl_ref[m, lv * _SUBLANES:(lv + 1) * _SUBLANES, _HALF:_STATE_W]
            sr = jnp.where(rows >= s, pltpu.roll(zr, s, 0), 0.0)
            si = jnp.where(rows >= s, pltpu.roll(zi, s, 0), 0.0)
            tr, ti = _cmul(ar, ai, sr, si)
            zr = zr + tr
            zi = zi + ti
        pr = carry_ref[m, :, 0:_HALF]
        pi = carry_ref[m, :, _HALF:_STATE_W]
        tr, ti = _cmul(cp_ref[m, :, 0:_HALF], cp_ref[m, :, _HALF:_STATE_W], pr, pi)
        er = zr + tr
        ei = zi + ti
        cr = jnp.where(rows >= 1, pltpu.roll(er, 1, 0), pr)
        ci = jnp.where(rows >= 1, pltpu.roll(ei, 1, 0), pi)
        carry_ref[m, :, 0:_HALF] = jnp.broadcast_to(er[_SUBLANES - 1:_SUBLANES, :], (_SUBLANES, _HALF))
        carry_ref[m, :, _HALF:_STATE_W] = jnp.broadcast_to(ei[_SUBLANES - 1:_SUBLANES, :], (_SUBLANES, _HALF))

        for tau in range(0, lc, 2):
            r0 = tau * _SUBLANES
            cr, ci = _cmul(lr, li, cr, ci)
            x0r = bu_ref[r0:r0 + _SUBLANES, 0:_HALF] + cr
            x0i = bu_ref[r0:r0 + _SUBLANES, _HALF:_STATE_W] + ci
            cr, ci = _cmul(lr, li, cr, ci)
            x1r = bu_ref[r0 + _SUBLANES:r0 + 2 * _SUBLANES, 0:_HALF] + cr
            x1i = bu_ref[r0 + _SUBLANES:r0 + 2 * _SUBLANES, _HALF:_STATE_W] + ci
            xb_ref[r0:r0 + 2 * _SUBLANES, 0:_HALF] = jnp.concatenate([x0r, x1r], axis=0).astype(_BF16)
            xb_ref[r0:r0 + 2 * _SUBLANES, _HALF:_STATE_W] = jnp.concatenate([x0i, x1i], axis=0).astype(_BF16)

        y_ref[m] = jnp.dot(xb_ref[...], wc_ref[m], preferred_element_type=_F32)
        return c

    lax.fori_loop(0, n_tiles, m_body, 0)

    y = jnp.concatenate([y_ref[m] for m in range(n_tiles)], axis=1)
    y_hi = y.astype(_BF16)
    y_lo = (y - y_hi.astype(_F32)).astype(_BF16)
    pt = pt_ref[...]
    yn = (jnp.dot(pt, y_hi, preferred_element_type=_F32)
          + jnp.dot(pt, y_lo, preferred_element_type=_F32))
    yn = yn + dsk_ref[...] * h_ref[...]
    cdf = 0.5 * (1.0 + jnp.tanh(math.sqrt(2.0 / math.pi) * (yn + 0.044715 * (yn * yn * yn))))
    z_ref[...] = (yn * cdf).astype(z_ref.dtype)


def _s5_tables(lam_re, lam_im, log_dt, b_re, b_im, c_re, c_im):
    g, p = lam_re.shape
    c = b_re.shape[-1]
    nm = g // _GROUPS_PER_TILE
    lr = lam_re.astype(_F32)
    li = lam_im.astype(_F32)
    dt = jnp.exp(log_dt.astype(_F32))[:, None]
    mag = jnp.exp(lr * dt)
    ang = li * dt
    lb_re = mag * jnp.cos(ang)
    lb_im = mag * jnp.sin(ang)
    nr = lb_re - 1.0
    den = lr * lr + li * li
    f_re = (nr * lr + lb_im * li) / den
    f_im = (lb_im * lr - nr * li) / den
    br = b_re.astype(_F32)
    bi = b_im.astype(_F32)
    bb_re = f_re[..., None] * br - f_im[..., None] * bi
    bb_im = f_re[..., None] * bi + f_im[..., None] * br
    eye = jnp.eye(_GROUPS_PER_TILE, dtype=_F32)

    def blk_b(bb):
        t = bb.reshape(nm, _GROUPS_PER_TILE, p, c)
        return jnp.einsum('mgpc,gh->mgchp', t, eye).reshape(nm, _GROUPS_PER_TILE * c, _GROUPS_PER_TILE * p)

    def blk_c(cc):
        t = cc.reshape(nm, _GROUPS_PER_TILE, c, p)
        return jnp.einsum('mgcp,gh->mgphc', t, eye).reshape(nm, _GROUPS_PER_TILE * p, _GROUPS_PER_TILE * c)

    wb = jnp.concatenate([blk_b(bb_re), blk_b(bb_im)], axis=-1).astype(_BF16)
    wc = jnp.concatenate([blk_c(c_re.astype(_F32)), blk_c(-c_im.astype(_F32))], axis=1).astype(_BF16)
    lam = jnp.concatenate([lb_re.reshape(nm, _HALF), lb_im.reshape(nm, _HALF)], axis=-1)
    lam8 = jnp.broadcast_to(lam[:, None, :], (nm, _SUBLANES, _STATE_W))
    return lam8, wb, wc


def _s5_mixer(x2, bsz, seq, gain, d_skip, lam8, wb, wc, *, tile=256):
    t_tok, d = x2.shape
    nm = lam8.shape[0]
    lc = tile // _SUBLANES
    nt = seq // tile
    r = jnp.arange(tile)
    col = (r % _SUBLANES) * lc + r // _SUBLANES
    perm = (col[:, None] == jnp.arange(tile)[None, :]).astype(_BF16)
    perm_t = perm.T
    const3 = lambda b, t: (0, 0, 0)
    const2 = lambda b, t: (0, 0)
    return pl.pallas_call(
        functools.partial(_s5_kernel, lc=lc),
        grid=(bsz, nt),
        in_specs=[
            pl.BlockSpec((tile, d), lambda b, t: (b * nt + t, 0)),
            pl.BlockSpec((1, d), const2),
            pl.BlockSpec((1, d), const2),
            pl.BlockSpec((nm, _SUBLANES, _STATE_W), const3),
            pl.BlockSpec((nm, _LANES, _STATE_W), const3),
            pl.BlockSpec((nm, _STATE_W, _LANES), const3),
            pl.BlockSpec((tile, tile), const2),
            pl.BlockSpec((tile, tile), const2),
        ],
        out_specs=pl.BlockSpec((tile, d), lambda b, t: (b * nt + t, 0)),
        out_shape=jax.ShapeDtypeStruct((t_tok, d), _BF16),
        scratch_shapes=[
            pltpu.VMEM((tile, d), _F32),
            pltpu.VMEM((nm, tile, _LANES), _BF16),
            pltpu.VMEM((nm, tile, _LANES), _F32),
            pltpu.VMEM((tile, _STATE_W), _F32),
            pltpu.VMEM((tile, _STATE_W), _BF16),
            pltpu.VMEM((nm, _SUBLANES, _STATE_W), _F32),
            pltpu.VMEM((nm, _SUBLANES, _STATE_W), _F32),
            pltpu.VMEM((nm, 3 * _SUBLANES, _STATE_W), _F32),
        ],
        compiler_params=_params(("arbitrary", "arbitrary")),
        name="s5_mixer",
    )(x2, gain.reshape(1, d).astype(_F32), d_skip.reshape(1, d).astype(_F32), lam8, wb, wc, perm, perm_t)


def _glu_kernel(z_ref, wv_ref, wg_ref, x_ref, o_ref):
    z = z_ref[...]
    val = jnp.dot(z, wv_ref[...], preferred_element_type=_F32)
    gate = jnp.dot(z, wg_ref[...], preferred_element_type=_F32)
    o_ref[...] = x_ref[...] + val * _sigmoid(gate)


def _glu_residual(z, w_glu, layer, x2, *, tm=1024, tn=512):
    t_tok, d = x2.shape
    nj = d // tn
    return pl.pallas_call(
        _glu_kernel,
        grid=(t_tok // tm, nj),
        in_specs=[
            pl.BlockSpec((tm, d), lambda i, j: (i, 0)),
            pl.BlockSpec((None, d, tn), lambda i, j: (layer, 0, j)),
            pl.BlockSpec((None, d, tn), lambda i, j: (layer, 0, j + nj)),
            pl.BlockSpec((tm, tn), lambda i, j: (i, j)),
        ],
        out_specs=pl.BlockSpec((tm, tn), lambda i, j: (i, j)),
        out_shape=jax.ShapeDtypeStruct((t_tok, d), _F32),
        compiler_params=_params(("parallel", "arbitrary")),
        name="glu_residual",
    )(z, w_glu, w_glu, x2)


def _ffn_kernel(x_ref, g_ref, wg_ref, wu_ref, wo_ref, fg_ref, o_ref, hn_ref, *, final_norm):
    f = pl.program_id(1)

    @pl.when(f == 0)
    def _start():
        x = x_ref[...]
        hn_ref[...] = (x * _rms_scale(x) * g_ref[...]).astype(_BF16)
        o_ref[...] = x

    hn = hn_ref[...]
    a = jnp.dot(hn, wg_ref[...], preferred_element_type=_F32)
    u = jnp.dot(hn, wu_ref[...], preferred_element_type=_F32)
    act = (a * _sigmoid(a) * u).astype(_BF16)
    o_ref[...] += jnp.dot(act, wo_ref[...], preferred_element_type=_F32)

    if final_norm:
        @pl.when(f == pl.num_programs(1) - 1)
        def _finish():
            y = o_ref[...]
            o_ref[...] = y * _rms_scale(y) * fg_ref[...]


def _ffn_residual(x2, gain, w_in, w_out, layer, final_gain=None, *, tm=512, tf=512):
    t_tok, d = x2.shape
    hidden = w_out.shape[1]
    nf = hidden // tf
    final_norm = final_gain is not None
    fg = (final_gain if final_norm else gain).reshape(1, d).astype(_F32)
    return pl.pallas_call(
        functools.partial(_ffn_kernel, final_norm=final_norm),
        grid=(t_tok // tm, nf),
        in_specs=[
            pl.BlockSpec((tm, d), lambda i, f: (i, 0)),
            pl.BlockSpec((1, d), lambda i, f: (0, 0)),
            pl.BlockSpec((None, d, tf), lambda i, f: (layer, 0, f)),
            pl.BlockSpec((None, d, tf), lambda i, f: (layer, 0, f + nf)),
            pl.BlockSpec((None, tf, d), lambda i, f: (layer, f, 0)),
            pl.BlockSpec((1, d), lambda i, f: (0, 0)),
        ],
        out_specs=pl.BlockSpec((tm, d), lambda i, f: (i, 0)),
        out_shape=jax.ShapeDtypeStruct((t_tok, d), _F32),
        scratch_shapes=[pltpu.VMEM((tm, d), _BF16)],
        compiler_params=_params(("parallel", "arbitrary")),
        name="ffn_residual",
    )(x2, gain.reshape(1, d).astype(_F32), w_in, w_in, w_out, fg)


_PERM_ROWS = 256


def _perm_matrix(dil, rows=_PERM_ROWS):
    dst = jnp.arange(rows)
    src = (dst % (rows // dil)) * dil + dst // (rows // dil)
    return (src[:, None] == jnp.arange(rows)[None, :]).astype(_BF16)


def _to_residue_major(y, p_ref, dil, store):
    tm = y.shape[0]
    run = _PERM_ROWS // dil
    for j in range(tm // _PERM_ROWS):
        yp = jnp.dot(p_ref[...], y[j * _PERM_ROWS:(j + 1) * _PERM_ROWS], preferred_element_type=_F32).astype(_BF16)
        for r in range(dil):
            store(r, j * run, yp[r * run:(r + 1) * run])


def _kv_kernel(*refs, dils):
    n_p = sum(1 for dl in dils if dl > 1)
    x_ref, g_ref, w_ref = refs[:3]
    p_refs = refs[3:3 + n_p]
    o_refs = refs[3 + n_p:]
    x = x_ref[...]
    hn = (x * _rms_scale(x) * g_ref[...]).astype(_BF16)
    kv = jnp.dot(hn, w_ref[...], preferred_element_type=_F32).astype(_BF16)
    pi = 0
    for o_ref, dl in zip(o_refs, dils):
        if dl == 1:
            o_ref[0] = kv
        else:
            def store(r, row, block, o_ref=o_ref):
                o_ref[r, row:row + block.shape[0], :] = block
            _to_residue_major(kv, p_refs[pi], dl, store)
            pi += 1


def _kv_project(x2, bsz, seq, gain, w, dils, *, tm=512):
    t_tok, d = x2.shape
    n = w.shape[1]
    nt = seq // tm
    perms = [_perm_matrix(dl) for dl in dils if dl > 1]
    return pl.pallas_call(
        functools.partial(_kv_kernel, dils=tuple(dils)),
        grid=(bsz, nt),
        in_specs=([pl.BlockSpec((tm, d), lambda b, t: (b * nt + t, 0)),
                   pl.BlockSpec((1, d), lambda b, t: (0, 0)),
                   pl.BlockSpec((d, n), lambda b, t: (0, 0))]
                  + [pl.BlockSpec((_PERM_ROWS, _PERM_ROWS), lambda b, t: (0, 0))] * len(perms)),
        out_specs=[pl.BlockSpec((None, dl, tm // dl, n), lambda b, t: (b, 0, t, 0)) for dl in dils],
        out_shape=[jax.ShapeDtypeStruct((bsz, dl, seq // dl, n), _BF16) for dl in dils],
        compiler_params=_params(("parallel", "arbitrary")),
        name="kv_project",
    )(x2, gain.reshape(1, d).astype(_F32), w, *perms)


def _q_kernel(*refs, dil):
    if dil > 1:
        x_ref, g_ref, w_ref, p_ref, o_ref, hn_ref = refs
    else:
        x_ref, g_ref, w_ref, o_ref, hn_ref = refs
    tm = hn_ref.shape[0]

    @pl.when(pl.program_id(2) == 0)
    def _start():
        x = x_ref[...]
        hn = (x * _rms_scale(x) * g_ref[...]).astype(_BF16)
        if dil > 1:
            def store(r, row, block):
                hn_ref[r * (tm // dil) + row:r * (tm // dil) + row + block.shape[0], :] = block
            _to_residue_major(hn, p_ref, dil, store)
        else:
            hn_ref[...] = hn

    res = jnp.dot(hn_ref[...], w_ref[...], preferred_element_type=_F32).astype(o_ref.dtype)
    for r in range(dil):
        o_ref[r] = res[r * (tm // dil):(r + 1) * (tm // dil)]


def _q_project(x2, bsz, seq, gain, w_q, layer, group, n_groups, dil, *, tm=512, tn=1024):
    t_tok, d = x2.shape
    qw = w_q.shape[2] // n_groups
    nt = seq // tm
    nj = qw // tn
    perms = [_perm_matrix(dil)] if dil > 1 else []
    return pl.pallas_call(
        functools.partial(_q_kernel, dil=dil),
        grid=(bsz, nt, nj),
        in_specs=([pl.BlockSpec((tm, d), lambda b, t, j: (b * nt + t, 0)),
                   pl.BlockSpec((1, d), lambda b, t, j: (0, 0)),
                   pl.BlockSpec((None, d, tn), lambda b, t, j: (layer, 0, group * nj + j))]
                  + [pl.BlockSpec((_PERM_ROWS, _PERM_ROWS), lambda b, t, j: (0, 0))] * len(perms)),
        out_specs=pl.BlockSpec((None, dil, tm // dil, tn), lambda b, t, j: (b, 0, t, j)),
        out_shape=jax.ShapeDtypeStruct((bsz, dil, seq // dil, qw), _BF16),
        scratch_shapes=[pltpu.VMEM((tm, d), _BF16)],
        compiler_params=_params(("parallel", "parallel", "arbitrary")),
        name=f"q_project_d{dil}",
    )(x2, gain.reshape(1, d).astype(_F32), w_q, *perms)


def _attn_kernel(q_ref, kp_ref, kc_ref, vp_ref, vc_ref, o_ref, st_ref, *, blk, rep, scale):
    i = pl.program_id(2)
    qi = lax.broadcasted_iota(jnp.int32, (blk, 2 * blk), 0)
    si = lax.broadcasted_iota(jnp.int32, (blk, 2 * blk), 1)
    dist = qi + blk - si
    valid = (dist >= 0) & (dist <= blk) & ((i > 0) | (si >= blk))
    valid = jnp.concatenate([valid] * rep, axis=0)
    lane = lax.broadcasted_iota(jnp.int32, (blk, _LANES), 1)
    st = jnp.zeros((blk, _LANES), _F32)
    n_kv = kc_ref.shape[1] // _HEAD_DIM
    for kvh in range(n_kv):
        ks = slice(kvh * _HEAD_DIM, (kvh + 1) * _HEAD_DIM)
        k = jnp.concatenate([kp_ref[:, ks], kc_ref[:, ks]], axis=0)
        v = jnp.concatenate([vp_ref[:, ks], vc_ref[:, ks]], axis=0)
        q = jnp.concatenate(
            [q_ref[:, (kvh * rep + rp) * _HEAD_DIM:(kvh * rep + rp + 1) * _HEAD_DIM] for rp in range(rep)],
            axis=0)
        s = lax.dot_general(q, k, (((1,), (1,)), ((), ())), preferred_element_type=_F32) * scale
        s = jnp.where(valid, s, _NEG_INF)
        m = jnp.max(s, axis=-1, keepdims=True)
        p = jnp.exp(s - m)
        l = jnp.sum(p, axis=-1, keepdims=True)
        pn = (p * (1.0 / l)).astype(_BF16)
        o = jnp.dot(pn, v, preferred_element_type=_F32)
        lse = m + jnp.log(l)
        for rp in range(rep):
            hd = kvh * rep + rp
            o_ref[:, hd * _HEAD_DIM:(hd + 1) * _HEAD_DIM] = o[rp * blk:(rp + 1) * blk].astype(o_ref.dtype)
            st = jnp.where(lane == hd, lse[rp * blk:(rp + 1) * blk], st)
    st_ref[...] = st


def _dilated_attention(q, kv, window, dilation):
    bsz, dil, n, qw = q.shape
    assert dil == dilation
    kw = kv.shape[3] // 2
    n_heads = qw // _HEAD_DIM
    rep = n_heads // (kw // _HEAD_DIM)
    blk = window // dilation
    assert n % blk == 0 and blk == _LANES and n_heads <= _LANES
    nb = n // blk
    prev = lambda i: jnp.maximum(i - 1, 0)
    return pl.pallas_call(
        functools.partial(_attn_kernel, blk=blk, rep=rep, scale=_HEAD_DIM ** -0.5),
        grid=(bsz, dilation, nb),
        in_specs=[
            pl.BlockSpec((None, None, blk, qw), lambda b, r, i: (b, r, i, 0)),
            pl.BlockSpec((None, None, blk, kw), lambda b, r, i: (b, r, prev(i), 0)),
            pl.BlockSpec((None, None, blk, kw), lambda b, r, i: (b, r, i, 0)),
            pl.BlockSpec((None, None, blk, kw), lambda b, r, i: (b, r, prev(i), 1)),
            pl.BlockSpec((None, None, blk, kw), lambda b, r, i: (b, r, i, 1)),
        ],
        out_specs=[
            pl.BlockSpec((None, None, blk, qw), lambda b, r, i: (b, r, i, 0)),
            pl.BlockSpec((None, None, blk, _LANES), lambda b, r, i: (b, r, i, 0)),
        ],
        out_shape=[
            jax.ShapeDtypeStruct((bsz, dilation, n, qw), _BF16),
            jax.ShapeDtypeStruct((bsz, dilation, n, _LANES), _F32),
        ],
        compiler_params=_params(("parallel", "parallel", "arbitrary")),
        name=f"dilated_attention_d{dilation}",
    )(q, kv, kv, kv, kv)


def _combine_kernel(*refs, dils, n_heads):
    n_groups = len(dils)
    n_p = sum(1 for dl in dils if dl > 1)
    o_in = refs[:n_groups]
    s_in = refs[n_groups:2 * n_groups]
    x_ref, wo_ref = refs[2 * n_groups:2 * n_groups + 2]
    pt_refs = refs[2 * n_groups + 2:2 * n_groups + 2 + n_p]
    out_ref, onat_ref, snat_ref, comb_ref = refs[2 * n_groups + 2 + n_p:]
    tm = x_ref.shape[0]

    pi = 0
    for g, dl in enumerate(dils):
        if dl == 1:
            onat_ref[g] = o_in[g][0]
            snat_ref[g] = s_in[g][0]
            continue
        run = _PERM_ROWS // dl
        pt = pt_refs[pi][...]
        pi += 1
        for j in range(tm // _PERM_ROWS):
            rs = slice(j * _PERM_ROWS, (j + 1) * _PERM_ROWS)
            oc = jnp.concatenate([o_in[g][r, j * run:(j + 1) * run, :] for r in range(dl)], axis=0)
            onat_ref[g, rs, :] = jnp.dot(pt, oc, preferred_element_type=_F32).astype(_BF16)
            sc = jnp.concatenate([s_in[g][r, j * run:(j + 1) * run, :] for r in range(dl)], axis=0)
            s_hi = sc.astype(_BF16)
            rem = sc - s_hi.astype(_F32)
            s_mid = rem.astype(_BF16)
            s_lo = (rem - s_mid.astype(_F32)).astype(_BF16)
            snat_ref[g, rs, :] = (jnp.dot(pt, s_hi, preferred_element_type=_F32)
                                  + jnp.dot(pt, s_mid, preferred_element_type=_F32)
                                  + jnp.dot(pt, s_lo, preferred_element_type=_F32))

    o_refs = [onat_ref.at[g] for g in range(n_groups)]
    lses = [snat_ref[g] for g in range(n_groups)]
    mx = lses[0]
    for l in lses[1:]:
        mx = jnp.maximum(mx, l)
    es = [jnp.exp(l - mx) for l in lses]
    den = es[0]
    for e in es[1:]:
        den = den + e
    inv = 1.0 / den
    wts = [e * inv for e in es]
    for hd in range(n_heads):
        cs = slice(hd * _HEAD_DIM, (hd + 1) * _HEAD_DIM)
        acc = wts[0][:, hd:hd + 1] * o_refs[0][:, cs].astype(_F32)
        for g in range(1, n_groups):
            acc = acc + wts[g][:, hd:hd + 1] * o_refs[g][:, cs].astype(_F32)
        comb_ref[:, cs] = acc.astype(_BF16)
    out_ref[...] = x_ref[...] + jnp.dot(comb_ref[...], wo_ref[...], preferred_element_type=_F32)


def _combine_project(outs, stats, x2, w_o, layer, dils, *, tm=512):
    t_tok, d = x2.shape
    bsz, _, _, qw = outs[0].shape
    seq = t_tok // bsz
    nt = seq // tm
    n_groups = len(outs)
    perms_t = [_perm_matrix(dl).T for dl in dils if dl > 1]
    grp = lambda b, t: (b, 0, t, 0)
    return pl.pallas_call(
        functools.partial(_combine_kernel, dils=tuple(dils), n_heads=qw // _HEAD_DIM),
        grid=(bsz, nt),
        in_specs=([pl.BlockSpec((None, dl, tm // dl, qw), grp) for dl in dils]
                  + [pl.BlockSpec((None, dl, tm // dl, _LANES), grp) for dl in dils]
                  + [pl.BlockSpec((tm, d), lambda b, t: (b * nt + t, 0)),
                     pl.BlockSpec((None, qw, d), lambda b, t: (layer, 0, 0))]
                  + [pl.BlockSpec((_PERM_ROWS, _PERM_ROWS), lambda b, t: (0, 0))] * len(perms_t)),
        out_specs=pl.BlockSpec((tm, d), lambda b, t: (b * nt + t, 0)),
        out_shape=jax.ShapeDtypeStruct((t_tok, d), _F32),
        scratch_shapes=[pltpu.VMEM((n_groups, tm, qw), _BF16),
                        pltpu.VMEM((n_groups, tm, _LANES), _F32),
                        pltpu.VMEM((tm, qw), _BF16)],
        compiler_params=_params(("parallel", "arbitrary")),
        name="combine_project",
    )(*outs, *stats, x2, w_o, *perms_t)


def kernel(x, s5_lam_re, s5_lam_im, s5_log_dt, s5_b_re, s5_b_im, s5_c_re, s5_c_im, s5_d, s5_w_glu, a_norm_mix, ffn_norm, ffn_w_in, ffn_w_out, b_norm_mix, attn_w_q, attn_w_o, kv_norm, w_kv, final_norm):
    bsz, seq, d = x.shape
    n_a = a_norm_mix.shape[0]
    n_b = b_norm_mix.shape[0]
    depth = n_a + n_b
    n_groups = len(_DILATED_PATTERNS)
    dils = [dl for _, dl in _DILATED_PATTERNS]
    x2 = x.reshape(bsz * seq, d).astype(_F32)
    w_glu_b = s5_w_glu.astype(_BF16)
    w_in_b = ffn_w_in.astype(_BF16)
    w_out_b = ffn_w_out.astype(_BF16)
    w_q_b = attn_w_q.astype(_BF16)
    w_o_b = attn_w_o.astype(_BF16)
    kvs = None
    for layer in range(depth):
        if layer < n_a:
            i = layer
            lam8, wb, wc = _s5_tables(s5_lam_re[i], s5_lam_im[i], s5_log_dt[i], s5_b_re[i], s5_b_im[i],
                                      s5_c_re[i], s5_c_im[i])
            z = _s5_mixer(x2, bsz, seq, a_norm_mix[i], s5_d[i], lam8, wb, wc)
            x2 = _glu_residual(z, w_glu_b, i, x2)
        else:
            j = layer - n_a
            if j == 0:
                kvs = _kv_project(x2, bsz, seq, kv_norm, w_kv.astype(_BF16), dils)
            outs, stats = [], []
            for g, (window, dilation) in enumerate(_DILATED_PATTERNS):
                q = _q_project(x2, bsz, seq, b_norm_mix[j], w_q_b, j, g, n_groups, dilation)
                o, st = _dilated_attention(q, kvs[g], window, dilation)
                outs.append(o)
                stats.append(st)
            x2 = _combine_project(outs, stats, x2, w_o_b, j, dils)
        final_gain = final_norm if layer == depth - 1 else None
        x2 = _ffn_residual(x2, ffn_norm[layer], w_in_b, w_out_b, layer, final_gain)
    return x2.reshape(bsz, seq, d).astype(x.dtype)
```

```python
import functools
import math

import jax
import jax.numpy as jnp
from jax import lax
from jax.experimental import pallas as pl
from jax.experimental.pallas import tpu as pltpu

_F32 = jnp.float32
_BF16 = jnp.bfloat16
_EPS = 1e-6
_NEG_INF = -1e30

_LANES = 128
_SUBLANES = 8
_VMEM_LIMIT_BYTES = 60 * 1024 * 1024

_S5_GROUP_CH = 16
_S5_STATE = 64
_HEAD_DIM = 128
_N_KV_HEADS = 4
_DILATED_PATTERNS = ((128, 1), (512, 4), (2048, 16))
_GROUPS_PER_TILE = _LANES // _S5_GROUP_CH
_HALF = _GROUPS_PER_TILE * _S5_STATE
_STATE_W = 2 * _HALF


def _params(sem):
    return pltpu.CompilerParams(dimension_semantics=sem, vmem_limit_bytes=_VMEM_LIMIT_BYTES)


def _rms_scale(x):
    return lax.rsqrt(jnp.mean(x * x, axis=-1, keepdims=True) + _EPS)


def _sigmoid(v):
    return 1.0 / (1.0 + jnp.exp(-v))


def _cmul(ar, ai, br, bi):
    return ar * br - ai * bi, ar * bi + ai * br


def _s5_kernel(x_ref, g_ref, dsk_ref, lam_ref, wb_ref, wc_ref, p_ref, pt_ref, z_ref,
               h_ref, hbp_ref, y_ref, bu_ref, xb_ref, carry_ref, cp_ref, lvl_ref, *, lc):
    t = pl.program_id(1)
    n_tiles = lam_ref.shape[0]
    rows = lax.broadcasted_iota(jnp.int32, (_SUBLANES, _HALF), 0)

    @pl.when(t == 0)
    def _init():
        carry_ref[...] = jnp.zeros_like(carry_ref)

        def init_m(m, c):
            pr = lam_ref[m, :, 0:_HALF]
            pi = lam_ref[m, :, _HALF:_STATE_W]
            for _ in range(int(math.log2(lc))):
                pr, pi = _cmul(pr, pi, pr, pi)
            qr, qi = pr, pi
            for lv in range(3):
                lvl_ref[m, lv * _SUBLANES:(lv + 1) * _SUBLANES, 0:_HALF] = qr
                lvl_ref[m, lv * _SUBLANES:(lv + 1) * _SUBLANES, _HALF:_STATE_W] = qi
                qr, qi = _cmul(qr, qi, qr, qi)
            cr, ci = pr, pi
            outr, outi = pr, pi
            for c_idx in range(1, _SUBLANES):
                cr, ci = _cmul(cr, ci, pr, pi)
                outr = jnp.where(rows == c_idx, cr, outr)
                outi = jnp.where(rows == c_idx, ci, outi)
            cp_ref[m, :, 0:_HALF] = outr
            cp_ref[m, :, _HALF:_STATE_W] = outi
            return c

        lax.fori_loop(0, n_tiles, init_m, 0)

    x = x_ref[...]
    h = x * _rms_scale(x) * g_ref[...]
    h_ref[...] = h
    hp = jnp.dot(p_ref[...], h.astype(_BF16), preferred_element_type=_F32)
    for m in range(n_tiles):
        hbp_ref[m] = hp[:, m * _LANES:(m + 1) * _LANES].astype(_BF16)

    def m_body(m, c):
        bu_ref[...] = jnp.dot(hbp_ref[m], wb_ref[m], preferred_element_type=_F32)
        lr = lam_ref[m, :, 0:_HALF]
        li = lam_ref[m, :, _HALF:_STATE_W]

        xr = jnp.zeros((_SUBLANES, _HALF), _F32)
        xi = jnp.zeros((_SUBLANES, _HALF), _F32)
        for tau in range(lc):
            r0 = tau * _SUBLANES
            tr, ti = _cmul(lr, li, xr, xi)
            xr = tr + bu_ref[r0:r0 + _SUBLANES, 0:_HALF]
            xi = ti + bu_ref[r0:r0 + _SUBLANES, _HALF:_STATE_W]
            bu_ref[r0:r0 + _SUBLANES, 0:_HALF] = xr
            bu_ref[r0:r0 + _SUBLANES, _HALF:_STATE_W] = xi

        zr, zi = xr, xi
        for lv, s in enumerate((1, 2, 4)):
            ar = lvl_ref[m, lv * _SUBLANES:(lv + 1) * _SUBLANES, 0:_HALF]
            ai = lvl_ref[m, lv * _SUBLANES:(lv + 1) * _SUBLANES, _HALF:_STATE_W]
            sr = jnp.where(rows >= s, pltpu.roll(zr, s, 0), 0.0)
            si = jnp.where(rows >= s, pltpu.roll(zi, s, 0), 0.0)
            tr, ti = _cmul(ar, ai, sr, si)
            zr = zr + tr
            zi = zi + ti
        pr = carry_ref[m, :, 0:_HALF]
        pi = carry_ref[m, :, _HALF:_STATE_W]
        tr, ti = _cmul(cp_ref[m, :, 0:_HALF], cp_ref[m, :, _HALF:_STATE_W], pr, pi)
        er = zr + tr
        ei = zi + ti
        cr = jnp.where(rows >= 1, pltpu.roll(er, 1, 0), pr)
        ci = jnp.where(rows >= 1, pltpu.roll(ei, 1, 0), pi)
        carry_ref[m, :, 0:_HALF] = jnp.broadcast_to(er[_SUBLANES - 1:_SUBLANES, :], (_SUBLANES, _HALF))
        carry_ref[m, :, _HALF:_STATE_W] = jnp.broadcast_to(ei[_SUBLANES - 1:_SUBLANES, :], (_SUBLANES, _HALF))

        for tau in range(0, lc, 2):
            r0 = tau * _SUBLANES
            cr, ci = _cmul(lr, li, cr, ci)
            x0r = bu_ref[r0:r0 + _SUBLANES, 0:_HALF] + cr
            x0i = bu_ref[r0:r0 + _SUBLANES, _HALF:_STATE_W] + ci
            cr, ci = _cmul(lr, li, cr, ci)
            x1r = bu_ref[r0 + _SUBLANES:r0 + 2 * _SUBLANES, 0:_HALF] + cr
            x1i = bu_ref[r0 + _SUBLANES:r0 + 2 * _SUBLANES, _HALF:_STATE_W] + ci
            xb_ref[r0:r0 + 2 * _SUBLANES, 0:_HALF] = jnp.concatenate([x0r, x1r], axis=0).astype(_BF16)
            xb_ref[r0:r0 + 2 * _SUBLANES, _HALF:_STATE_W] = jnp.concatenate([x0i, x1i], axis=0).astype(_BF16)

        y_ref[m] = jnp.dot(xb_ref[...], wc_ref[m], preferred_element_type=_F32)
        return c

    lax.fori_loop(0, n_tiles, m_body, 0)

    y = jnp.concatenate([y_ref[m] for m in range(n_tiles)], axis=1)
    y_hi = y.astype(_BF16)
    y_lo = (y - y_hi.astype(_F32)).astype(_BF16)
    pt = pt_ref[...]
    yn = (jnp.dot(pt, y_hi, preferred_element_type=_F32)
          + jnp.dot(pt, y_lo, preferred_element_type=_F32))
    yn = yn + dsk_ref[...] * h_ref[...]
    cdf = 0.5 * (1.0 + jnp.tanh(math.sqrt(2.0 / math.pi) * (yn + 0.044715 * (yn * yn * yn))))
    z_ref[...] = (yn * cdf).astype(z_ref.dtype)


def _s5_tables(lam_re, lam_im, log_dt, b_re, b_im, c_re, c_im):
    g, p = lam_re.shape
    c = b_re.shape[-1]
    nm = g // _GROUPS_PER_TILE
    lr = lam_re.astype(_F32)
    li = lam_im.astype(_F32)
    dt = jnp.exp(log_dt.astype(_F32))[:, None]
    mag = jnp.exp(lr * dt)
    ang = li * dt
    lb_re = mag * jnp.cos(ang)
    lb_im = mag * jnp.sin(ang)
    nr = lb_re - 1.0
    den = lr * lr + li * li
    f_re = (nr * lr + lb_im * li) / den
    f_im = (lb_im * lr - nr * li) / den
    br = b_re.astype(_F32)
    bi = b_im.astype(_F32)
    bb_re = f_re[..., None] * br - f_im[..., None] * bi
    bb_im = f_re[..., None] * bi + f_im[..., None] * br
    eye = jnp.eye(_GROUPS_PER_TILE, dtype=_F32)

    def blk_b(bb):
        t = bb.reshape(nm, _GROUPS_PER_TILE, p, c)
        return jnp.einsum('mgpc,gh->mgchp', t, eye).reshape(nm, _GROUPS_PER_TILE * c, _GROUPS_PER_TILE * p)

    def blk_c(cc):
        t = cc.reshape(nm, _GROUPS_PER_TILE, c, p)
        return jnp.einsum('mgcp,gh->mgphc', t, eye).reshape(nm, _GROUPS_PER_TILE * p, _GROUPS_PER_TILE * c)

    wb = jnp.concatenate([blk_b(bb_re), blk_b(bb_im)], axis=-1).astype(_BF16)
    wc = jnp.concatenate([blk_c(c_re.astype(_F32)), blk_c(-c_im.astype(_F32))], axis=1).astype(_BF16)
    lam = jnp.concatenate([lb_re.reshape(nm, _HALF), lb_im.reshape(nm, _HALF)], axis=-1)
    lam8 = jnp.broadcast_to(lam[:, None, :], (nm, _SUBLANES, _STATE_W))
    return lam8, wb, wc


def _s5_mixer(x2, bsz, seq, gain, d_skip, lam8, wb, wc, *, tile=256):
    t_tok, d = x2.shape
    nm = lam8.shape[0]
    lc = tile // _SUBLANES
    nt = seq // tile
    r = jnp.arange(tile)
    col = (r % _SUBLANES) * lc + r // _SUBLANES
    perm = (col[:, None] == jnp.arange(tile)[None, :]).astype(_BF16)
    perm_t = perm.T
    const3 = lambda b, t: (0, 0, 0)
    const2 = lambda b, t: (0, 0)
    return pl.pallas_call(
        functools.partial(_s5_kernel, lc=lc),
        grid=(bsz, nt),
        in_specs=[
            pl.BlockSpec((tile, d), lambda b, t: (b * nt + t, 0)),
            pl.BlockSpec((1, d), const2),
            pl.BlockSpec((1, d), const2),
            pl.BlockSpec((nm, _SUBLANES, _STATE_W), const3),
            pl.BlockSpec((nm, _LANES, _STATE_W), const3),
            pl.BlockSpec((nm, _STATE_W, _LANES), const3),
            pl.BlockSpec((tile, tile), const2),
            pl.BlockSpec((tile, tile), const2),
        ],
        out_specs=pl.BlockSpec((tile, d), lambda b, t: (b * nt + t, 0)),
        out_shape=jax.ShapeDtypeStruct((t_tok, d), _BF16),
        scratch_shapes=[
            pltpu.VMEM((tile, d), _F32),
            pltpu.VMEM((nm, tile, _LANES), _BF16),
            pltpu.VMEM((nm, tile, _LANES), _F32),
            pltpu.VMEM((tile, _STATE_W), _F32),
            pltpu.VMEM((tile, _STATE_W), _BF16),
            pltpu.VMEM((nm, _SUBLANES, _STATE_W), _F32),
            pltpu.VMEM((nm, _SUBLANES, _STATE_W), _F32),
            pltpu.VMEM((nm, 3 * _SUBLANES, _STATE_W), _F32),
        ],
        compiler_params=_params(("arbitrary", "arbitrary")),
        name="s5_mixer",
    )(x2, gain.reshape(1, d).astype(_F32), d_skip.reshape(1, d).astype(_F32), lam8, wb, wc, perm, perm_t)


def _glu_kernel(z_ref, wv_ref, wg_ref, x_ref, o_ref):
    z = z_ref[...]
    val = jnp.dot(z, wv_ref[...], preferred_element_type=_F32)
    gate = jnp.dot(z, wg_ref[...], preferred_element_type=_F32)
    o_ref[...] = x_ref[...] + val * _sigmoid(gate)


def _glu_residual(z, w_glu, layer, x2, *, tm=1024, tn=512):
    t_tok, d = x2.shape
    nj = d // tn
    return pl.pallas_call(
        _glu_kernel,
        grid=(t_tok // tm, nj),
        in_specs=[
            pl.BlockSpec((tm, d), lambda i, j: (i, 0)),
            pl.BlockSpec((None, d, tn), lambda i, j: (layer, 0, j)),
            pl.BlockSpec((None, d, tn), lambda i, j: (layer, 0, j + nj)),
            pl.BlockSpec((tm, tn), lambda i, j: (i, j)),
        ],
        out_specs=pl.BlockSpec((tm, tn), lambda i, j: (i, j)),
        out_shape=jax.ShapeDtypeStruct((t_tok, d), _F32),
        compiler_params=_params(("parallel", "arbitrary")),
        name="glu_residual",
    )(z, w_glu, w_glu, x2)


def _ffn_kernel(x_ref, g_ref, wg_ref, wu_ref, wo_ref, fg_ref, o_ref, hn_ref, *, final_norm, row_split):
    f = pl.program_id(1)
    rows = o_ref.shape[0] // row_split
    slabs = [slice(s * rows, (s + 1) * rows) for s in range(row_split)]

    @pl.when(f == 0)
    def _start():
        for sl in slabs:
            x = x_ref[sl, :]
            hn_ref[sl, :] = (x * _rms_scale(x) * g_ref[...]).astype(_BF16)
            o_ref[sl, :] = x

    for sl in slabs:
        hn = hn_ref[sl, :]
        a = jnp.dot(hn, wg_ref[...], preferred_element_type=_F32)
        u = jnp.dot(hn, wu_ref[...], preferred_element_type=_F32)
        act = (a * _sigmoid(a) * u).astype(_BF16)
        o_ref[sl, :] += jnp.dot(act, wo_ref[...], preferred_element_type=_F32)

    if final_norm:
        @pl.when(f == pl.num_programs(1) - 1)
        def _finish():
            for sl in slabs:
                y = o_ref[sl, :]
                o_ref[sl, :] = y * _rms_scale(y) * fg_ref[...]


def _ffn_residual(x2, gain, w_in, w_out, layer, final_gain=None, *, tm=1024, tf=512, row_split=2):
    t_tok, d = x2.shape
    hidden = w_out.shape[1]
    nf = hidden // tf
    final_norm = final_gain is not None
    fg = (final_gain if final_norm else gain).reshape(1, d).astype(_F32)
    return pl.pallas_call(
        functools.partial(_ffn_kernel, final_norm=final_norm, row_split=row_split),
        grid=(t_tok // tm, nf),
        in_specs=[
            pl.BlockSpec((tm, d), lambda i, f: (i, 0)),
            pl.BlockSpec((1, d), lambda i, f: (0, 0)),
            pl.BlockSpec((None, d, tf), lambda i, f: (layer, 0, f)),
            pl.BlockSpec((None, d, tf), lambda i, f: (layer, 0, f + nf)),
            pl.BlockSpec((None, tf, d), lambda i, f: (layer, f, 0)),
            pl.BlockSpec((1, d), lambda i, f: (0, 0)),
        ],
        out_specs=pl.BlockSpec((tm, d), lambda i, f: (i, 0)),
        out_shape=jax.ShapeDtypeStruct((t_tok, d), _F32),
        scratch_shapes=[pltpu.VMEM((tm, d), _BF16)],
        compiler_params=_params(("parallel", "arbitrary")),
        name="ffn_residual",
    )(x2, gain.reshape(1, d).astype(_F32), w_in, w_in, w_out, fg)


_PERM_ROWS = 256


def _perm_matrix(dil, rows=_PERM_ROWS):
    dst = jnp.arange(rows)
    src = (dst % (rows // dil)) * dil + dst // (rows // dil)
    return (src[:, None] == jnp.arange(rows)[None, :]).astype(_BF16)


def _to_residue_major(y, p_ref, dil, store):
    tm = y.shape[0]
    run = _PERM_ROWS // dil
    for j in range(tm // _PERM_ROWS):
        yp = jnp.dot(p_ref[...], y[j * _PERM_ROWS:(j + 1) * _PERM_ROWS], preferred_element_type=_F32).astype(_BF16)
        for r in range(dil):
            store(r, j * run, yp[r * run:(r + 1) * run])


def _qkv_kernel(*refs, dils, nj):
    n_groups = len(dils)
    n_p = sum(1 for dl in dils if dl > 1)
    x_ref, gq_ref, gkv_ref, w_ref = refs[:4]
    p_refs = refs[4:4 + n_p]
    q_refs = refs[4 + n_p:4 + n_p + n_groups]
    kv_refs = refs[4 + n_p + n_groups:4 + n_p + 2 * n_groups]
    hn_ref = refs[-1]
    tm = hn_ref.shape[1]
    p_of = {}
    for dl in dils:
        if dl > 1:
            p_of[dl] = p_refs[len(p_of)]
    j = pl.program_id(2)

    @pl.when(j == 0)
    def _start():
        x = x_ref[...]
        xr = x * _rms_scale(x)
        hq = (xr * gq_ref[...]).astype(_BF16)
        hn_ref[n_groups] = (xr * gkv_ref[...]).astype(_BF16)
        for g, dl in enumerate(dils):
            if dl == 1:
                hn_ref[g] = hq
            else:
                def store(r, row, block, g=g, dl=dl):
                    hn_ref[g, r * (tm // dl) + row:r * (tm // dl) + row + block.shape[0], :] = block
                _to_residue_major(hq, p_of[dl], dl, store)

    for g, dl in enumerate(dils):
        @pl.when((j >= g * nj) & (j < (g + 1) * nj))
        def _q_tile(g=g, dl=dl):
            res = jnp.dot(hn_ref[g], w_ref[...], preferred_element_type=_F32).astype(_BF16)
            for r in range(dl):
                q_refs[g][r] = res[r * (tm // dl):(r + 1) * (tm // dl)]

    @pl.when(j == n_groups * nj)
    def _kv_tile():
        kv = jnp.dot(hn_ref[n_groups], w_ref[...], preferred_element_type=_F32).astype(_BF16)
        for g, dl in enumerate(dils):
            if dl == 1:
                kv_refs[g][0] = kv
            else:
                def store(r, row, block, g=g):
                    kv_refs[g][r, row:row + block.shape[0], :] = block
                _to_residue_major(kv, p_of[dl], dl, store)


def _qkv_project(x2, bsz, seq, q_gain, kv_gain, w_cat, qw, dils, *, tm=512):
    t_tok, d = x2.shape
    n_groups = len(dils)
    kvw = w_cat.shape[1] - n_groups * qw
    tn = kvw
    assert qw % tn == 0
    nj = qw // tn
    nt = seq // tm
    perms = [_perm_matrix(dl) for dl in dils if dl > 1]

    def q_spec(g, dl):
        return pl.BlockSpec((None, dl, tm // dl, tn),
                            lambda b, t, j: (b, 0, t, jnp.clip(j - g * nj, 0, nj - 1)))

    outs = pl.pallas_call(
        functools.partial(_qkv_kernel, dils=tuple(dils), nj=nj),
        grid=(bsz, nt, n_groups * nj + 1),
        in_specs=([pl.BlockSpec((tm, d), lambda b, t, j: (b * nt + t, 0)),
                   pl.BlockSpec((1, d), lambda b, t, j: (0, 0)),
                   pl.BlockSpec((1, d), lambda b, t, j: (0, 0)),
                   pl.BlockSpec((d, tn), lambda b, t, j: (0, j))]
                  + [pl.BlockSpec((_PERM_ROWS, _PERM_ROWS), lambda b, t, j: (0, 0))] * len(perms)),
        out_specs=([q_spec(g, dl) for g, dl in enumerate(dils)]
                   + [pl.BlockSpec((None, dl, tm // dl, kvw), lambda b, t, j: (b, 0, t, 0)) for dl in dils]),
        out_shape=([jax.ShapeDtypeStruct((bsz, dl, seq // dl, qw), _BF16) for dl in dils]
                   + [jax.ShapeDtypeStruct((bsz, dl, seq // dl, kvw), _BF16) for dl in dils]),
        scratch_shapes=[pltpu.VMEM((n_groups + 1, tm, d), _BF16)],
        compiler_params=_params(("parallel", "parallel", "arbitrary")),
        name="qkv_project",
    )(x2, q_gain.reshape(1, d).astype(_F32), kv_gain.reshape(1, d).astype(_F32), w_cat, *perms)
    return outs[:n_groups], outs[n_groups:]


def _attn_kernel(q_ref, kp_ref, kc_ref, vp_ref, vc_ref, o_ref, st_ref, *, blk, rep, scale):
    i = pl.program_id(2)
    nq = q_ref.shape[0] // blk
    qi = lax.broadcasted_iota(jnp.int32, (blk, 2 * blk), 0)
    si = lax.broadcasted_iota(jnp.int32, (blk, 2 * blk), 1)
    dist = qi + blk - si
    band = (dist >= 0) & (dist <= blk)
    band_first = band & ((i > 0) | (si >= blk))
    lane = lax.broadcasted_iota(jnp.int32, (blk, _LANES), 1)
    n_kv = kc_ref.shape[1] // _HEAD_DIM
    for sub in range(nq):
        valid = jnp.concatenate([band_first if sub == 0 else band] * rep, axis=0)
        rows = slice(sub * blk, (sub + 1) * blk)
        st = jnp.zeros((blk, _LANES), _F32)
        for kvh in range(n_kv):
            ks = slice(kvh * _HEAD_DIM, (kvh + 1) * _HEAD_DIM)
            if sub == 0:
                k = jnp.concatenate([kp_ref[:, ks], kc_ref[0:blk, ks]], axis=0)
                v = jnp.concatenate([vp_ref[:, ks], vc_ref[0:blk, ks]], axis=0)
            else:
                k = kc_ref[(sub - 1) * blk:(sub + 1) * blk, ks]
                v = vc_ref[(sub - 1) * blk:(sub + 1) * blk, ks]
            q = jnp.concatenate(
                [q_ref[rows, (kvh * rep + rp) * _HEAD_DIM:(kvh * rep + rp + 1) * _HEAD_DIM] for rp in range(rep)],
                axis=0)
            s = lax.dot_general(q, k, (((1,), (1,)), ((), ())), preferred_element_type=_F32) * scale
            s = jnp.where(valid, s, _NEG_INF)
            m = jnp.max(s, axis=-1, keepdims=True)
            p = jnp.exp(s - m)
            l = jnp.sum(p, axis=-1, keepdims=True)
            pn = (p * (1.0 / l)).astype(_BF16)
            o = jnp.dot(pn, v, preferred_element_type=_F32)
            lse = m + jnp.log(l)
            for rp in range(rep):
                hd = kvh * rep + rp
                o_ref[rows, hd * _HEAD_DIM:(hd + 1) * _HEAD_DIM] = o[rp * blk:(rp + 1) * blk].astype(o_ref.dtype)
                st = jnp.where(lane == hd, lse[rp * blk:(rp + 1) * blk], st)
        st_ref[rows, :] = st


def _dilated_attention(q, kv, window, dilation, *, nq=2):
    bsz, dil, n, qw = q.shape
    assert dil == dilation
    kw = kv.shape[3] // 2
    n_heads = qw // _HEAD_DIM
    rep = n_heads // (kw // _HEAD_DIM)
    blk = window // dilation
    assert n % (nq * blk) == 0 and blk == _LANES and n_heads <= _LANES
    nb = n // (nq * blk)
    prev = lambda i: jnp.maximum(nq * i - 1, 0)
    return pl.pallas_call(
        functools.partial(_attn_kernel, blk=blk, rep=rep, scale=_HEAD_DIM ** -0.5),
        grid=(bsz, dilation, nb),
        in_specs=[
            pl.BlockSpec((None, None, nq * blk, qw), lambda b, r, i: (b, r, i, 0)),
            pl.BlockSpec((None, None, blk, kw), lambda b, r, i: (b, r, prev(i), 0)),
            pl.BlockSpec((None, None, nq * blk, kw), lambda b, r, i: (b, r, i, 0)),
            pl.BlockSpec((None, None, blk, kw), lambda b, r, i: (b, r, prev(i), 1)),
            pl.BlockSpec((None, None, nq * blk, kw), lambda b, r, i: (b, r, i, 1)),
        ],
        out_specs=[
            pl.BlockSpec((None, None, nq * blk, qw), lambda b, r, i: (b, r, i, 0)),
            pl.BlockSpec((None, None, nq * blk, _LANES), lambda b, r, i: (b, r, i, 0)),
        ],
        out_shape=[
            jax.ShapeDtypeStruct((bsz, dilation, n, qw), _BF16),
            jax.ShapeDtypeStruct((bsz, dilation, n, _LANES), _F32),
        ],
        compiler_params=_params(("parallel", "parallel", "arbitrary")),
        name=f"dilated_attention_d{dilation}",
    )(q, kv, kv, kv, kv)


def _combine_kernel(*refs, dils, n_heads):
    n_groups = len(dils)
    n_p = sum(1 for dl in dils if dl > 1)
    o_in = refs[:n_groups]
    s_in = refs[n_groups:2 * n_groups]
    x_ref, wo_ref = refs[2 * n_groups:2 * n_groups + 2]
    pt_refs = refs[2 * n_groups + 2:2 * n_groups + 2 + n_p]
    out_ref, onat_ref, snat_ref, comb_ref = refs[2 * n_groups + 2 + n_p:]
    tm = x_ref.shape[0]

    pi = 0
    for g, dl in enumerate(dils):
        if dl == 1:
            onat_ref[g] = o_in[g][0]
            snat_ref[g] = s_in[g][0]
            continue
        run = _PERM_ROWS // dl
        pt = pt_refs[pi][...]
        pi += 1
        for j in range(tm // _PERM_ROWS):
            rs = slice(j * _PERM_ROWS, (j + 1) * _PERM_ROWS)
            oc = jnp.concatenate([o_in[g][r, j * run:(j + 1) * run, :] for r in range(dl)], axis=0)
            onat_ref[g, rs, :] = jnp.dot(pt, oc, preferred_element_type=_F32).astype(_BF16)
            sc = jnp.concatenate([s_in[g][r, j * run:(j + 1) * run, :] for r in range(dl)], axis=0)
            s_hi = sc.astype(_BF16)
            rem = sc - s_hi.astype(_F32)
            s_mid = rem.astype(_BF16)
            s_lo = (rem - s_mid.astype(_F32)).astype(_BF16)
            snat_ref[g, rs, :] = (jnp.dot(pt, s_hi, preferred_element_type=_F32)
                                  + jnp.dot(pt, s_mid, preferred_element_type=_F32)
                                  + jnp.dot(pt, s_lo, preferred_element_type=_F32))

    o_refs = [onat_ref.at[g] for g in range(n_groups)]
    lses = [snat_ref[g] for g in range(n_groups)]
    mx = lses[0]
    for l in lses[1:]:
        mx = jnp.maximum(mx, l)
    es = [jnp.exp(l - mx) for l in lses]
    den = es[0]
    for e in es[1:]:
        den = den + e
    inv = 1.0 / den
    wts = [e * inv for e in es]
    for hd in range(n_heads):
        cs = slice(hd * _HEAD_DIM, (hd + 1) * _HEAD_DIM)
        acc = wts[0][:, hd:hd + 1] * o_refs[0][:, cs].astype(_F32)
        for g in range(1, n_groups):
            acc = acc + wts[g][:, hd:hd + 1] * o_refs[g][:, cs].astype(_F32)
        comb_ref[:, cs] = acc.astype(_BF16)
    out_ref[...] = x_ref[...] + jnp.dot(comb_ref[...], wo_ref[...], preferred_element_type=_F32)


def _combine_project(outs, stats, x2, w_o, layer, dils, *, tm=512):
    t_tok, d = x2.shape
    bsz, _, _, qw = outs[0].shape
    seq = t_tok // bsz
    nt = seq // tm
    n_groups = len(outs)
    perms_t = [_perm_matrix(dl).T for dl in dils if dl > 1]
    grp = lambda b, t: (b, 0, t, 0)
    return pl.pallas_call(
        functools.partial(_combine_kernel, dils=tuple(dils), n_heads=qw // _HEAD_DIM),
        grid=(bsz, nt),
        in_specs=([pl.BlockSpec((None, dl, tm // dl, qw), grp) for dl in dils]
                  + [pl.BlockSpec((None, dl, tm // dl, _LANES), grp) for dl in dils]
                  + [pl.BlockSpec((tm, d), lambda b, t: (b * nt + t, 0)),
                     pl.BlockSpec((None, qw, d), lambda b, t: (layer, 0, 0))]
                  + [pl.BlockSpec((_PERM_ROWS, _PERM_ROWS), lambda b, t: (0, 0))] * len(perms_t)),
        out_specs=pl.BlockSpec((tm, d), lambda b, t: (b * nt + t, 0)),
        out_shape=jax.ShapeDtypeStruct((t_tok, d), _F32),
        scratch_shapes=[pltpu.VMEM((n_groups, tm, qw), _BF16),
                        pltpu.VMEM((n_groups, tm, _LANES), _F32),
                        pltpu.VMEM((tm, qw), _BF16)],
        compiler_params=_params(("parallel", "arbitrary")),
        name="combine_project",
    )(*outs, *stats, x2, w_o, *perms_t)


def kernel(x, s5_lam_re, s5_lam_im, s5_log_dt, s5_b_re, s5_b_im, s5_c_re, s5_c_im, s5_d, s5_w_glu, a_norm_mix, ffn_norm, ffn_w_in, ffn_w_out, b_norm_mix, attn_w_q, attn_w_o, kv_norm, w_kv, final_norm):
    bsz, seq, d = x.shape
    n_a = a_norm_mix.shape[0]
    n_b = b_norm_mix.shape[0]
    depth = n_a + n_b
    n_groups = len(_DILATED_PATTERNS)
    dils = [dl for _, dl in _DILATED_PATTERNS]
    x2 = x.reshape(bsz * seq, d).astype(_F32)
    w_glu_b = s5_w_glu.astype(_BF16)
    w_in_b = ffn_w_in.astype(_BF16)
    w_out_b = ffn_w_out.astype(_BF16)
    w_o_b = attn_w_o.astype(_BF16)
    assert n_b == 1, "the fused q/kv projection covers the single attention layer of this model"
    qw = attn_w_q.shape[2] // n_groups
    for layer in range(depth):
        if layer < n_a:
            i = layer
            lam8, wb, wc = _s5_tables(s5_lam_re[i], s5_lam_im[i], s5_log_dt[i], s5_b_re[i], s5_b_im[i],
                                      s5_c_re[i], s5_c_im[i])
            z = _s5_mixer(x2, bsz, seq, a_norm_mix[i], s5_d[i], lam8, wb, wc)
            x2 = _glu_residual(z, w_glu_b, i, x2)
        else:
            j = layer - n_a
            w_cat = jnp.concatenate([attn_w_q[j], w_kv], axis=1).astype(_BF16)
            qs, kvs = _qkv_project(x2, bsz, seq, b_norm_mix[j], kv_norm, w_cat, qw, dils)
            outs, stats = [], []
            for g, (window, dilation) in enumerate(_DILATED_PATTERNS):
                o, st = _dilated_attention(qs[g], kvs[g], window, dilation)
                outs.append(o)
                stats.append(st)
            x2 = _combine_project(outs, stats, x2, w_o_b, j, dils)
        final_gain = final_norm if layer == depth - 1 else None
        x2 = _ffn_residual(x2, ffn_norm[layer], w_in_b, w_out_b, layer, final_gain)
    return x2.reshape(bsz, seq, d).astype(x.dtype)
```

```python
import functools
import math

import jax
import jax.numpy as jnp
from jax import lax
from jax.experimental import pallas as pl
from jax.experimental.pallas import tpu as pltpu

_F32 = jnp.float32
_BF16 = jnp.bfloat16
_EPS = 1e-6
_NEG_INF = -1e30

_LANES = 128
_SUBLANES = 8
_VMEM_LIMIT_BYTES = 60 * 1024 * 1024

_S5_GROUP_CH = 16
_S5_STATE = 64
_HEAD_DIM = 128
_N_KV_HEADS = 4
_DILATED_PATTERNS = ((128, 1), (512, 4), (2048, 16))
_GROUPS_PER_TILE = _LANES // _S5_GROUP_CH
_HALF = _GROUPS_PER_TILE * _S5_STATE
_STATE_W = 2 * _HALF


def _params(sem):
    return pltpu.CompilerParams(dimension_semantics=sem, vmem_limit_bytes=_VMEM_LIMIT_BYTES)


def _rms_scale(x):
    return lax.rsqrt(jnp.mean(x * x, axis=-1, keepdims=True) + _EPS)


def _sigmoid(v):
    return 1.0 / (1.0 + jnp.exp(-v))


def _cmul(ar, ai, br, bi):
    return ar * br - ai * bi, ar * bi + ai * br


def _s5_kernel(x_ref, g_ref, dsk_ref, lam_ref, wb_ref, wc_ref, p_ref, pt_ref, z_ref,
               h_ref, hbp_ref, y_ref, bu0_ref, bu1_ref, xb0_ref, xb1_ref, carry_ref, cp_ref, lvl_ref, *, lc):
    t = pl.program_id(1)
    n_tiles = lam_ref.shape[0]
    rows = lax.broadcasted_iota(jnp.int32, (_SUBLANES, _HALF), 0)

    @pl.when(t == 0)
    def _init():
        carry_ref[...] = jnp.zeros_like(carry_ref)

        def init_m(m, c):
            pr = lam_ref[m, :, 0:_HALF]
            pi = lam_ref[m, :, _HALF:_STATE_W]
            for _ in range(int(math.log2(lc))):
                pr, pi = _cmul(pr, pi, pr, pi)
            qr, qi = pr, pi
            for lv in range(3):
                lvl_ref[m, lv * _SUBLANES:(lv + 1) * _SUBLANES, 0:_HALF] = qr
                lvl_ref[m, lv * _SUBLANES:(lv + 1) * _SUBLANES, _HALF:_STATE_W] = qi
                qr, qi = _cmul(qr, qi, qr, qi)
            cr, ci = pr, pi
            outr, outi = pr, pi
            for c_idx in range(1, _SUBLANES):
                cr, ci = _cmul(cr, ci, pr, pi)
                outr = jnp.where(rows == c_idx, cr, outr)
                outi = jnp.where(rows == c_idx, ci, outi)
            cp_ref[m, :, 0:_HALF] = outr
            cp_ref[m, :, _HALF:_STATE_W] = outi
            return c

        lax.fori_loop(0, n_tiles, init_m, 0)

    x = x_ref[...]
    h = x * _rms_scale(x) * g_ref[...]
    h_ref[...] = h
    hp = jnp.dot(p_ref[...], h.astype(_BF16), preferred_element_type=_F32)
    for m in range(n_tiles):
        hbp_ref[m] = hp[:, m * _LANES:(m + 1) * _LANES].astype(_BF16)

    def project_in(m, bu_ref):
        bu_ref[...] = jnp.dot(hbp_ref[m], wb_ref[m], preferred_element_type=_F32)

    def project_out(m, xb_ref):
        y_ref[m] = jnp.dot(xb_ref[...], wc_ref[m], preferred_element_type=_F32)

    def scan(m, bu_ref, xb_ref):
        lr = lam_ref[m, :, 0:_HALF]
        li = lam_ref[m, :, _HALF:_STATE_W]

        xr = jnp.zeros((_SUBLANES, _HALF), _F32)
        xi = jnp.zeros((_SUBLANES, _HALF), _F32)
        for tau in range(lc):
            r0 = tau * _SUBLANES
            tr, ti = _cmul(lr, li, xr, xi)
            xr = tr + bu_ref[r0:r0 + _SUBLANES, 0:_HALF]
            xi = ti + bu_ref[r0:r0 + _SUBLANES, _HALF:_STATE_W]
            bu_ref[r0:r0 + _SUBLANES, 0:_HALF] = xr
            bu_ref[r0:r0 + _SUBLANES, _HALF:_STATE_W] = xi

        zr, zi = xr, xi
        for lv, s in enumerate((1, 2, 4)):
            ar = lvl_ref[m, lv * _SUBLANES:(lv + 1) * _SUBLANES, 0:_HALF]
            ai = lvl_ref[m, lv * _SUBLANES:(lv + 1) * _SUBLANES, _HALF:_STATE_W]
            sr = jnp.where(rows >= s, pltpu.roll(zr, s, 0), 0.0)
            si = jnp.where(rows >= s, pltpu.roll(zi, s, 0), 0.0)
            tr, ti = _cmul(ar, ai, sr, si)
            zr = zr + tr
            zi = zi + ti
        pr = carry_ref[m, :, 0:_HALF]
        pi = carry_ref[m, :, _HALF:_STATE_W]
        tr, ti = _cmul(cp_ref[m, :, 0:_HALF], cp_ref[m, :, _HALF:_STATE_W], pr, pi)
        er = zr + tr
        ei = zi + ti
        cr = jnp.where(rows >= 1, pltpu.roll(er, 1, 0), pr)
        ci = jnp.where(rows >= 1, pltpu.roll(ei, 1, 0), pi)
        carry_ref[m, :, 0:_HALF] = jnp.broadcast_to(er[_SUBLANES - 1:_SUBLANES, :], (_SUBLANES, _HALF))
        carry_ref[m, :, _HALF:_STATE_W] = jnp.broadcast_to(ei[_SUBLANES - 1:_SUBLANES, :], (_SUBLANES, _HALF))

        for tau in range(0, lc, 2):
            r0 = tau * _SUBLANES
            cr, ci = _cmul(lr, li, cr, ci)
            x0r = bu_ref[r0:r0 + _SUBLANES, 0:_HALF] + cr
            x0i = bu_ref[r0:r0 + _SUBLANES, _HALF:_STATE_W] + ci
            cr, ci = _cmul(lr, li, cr, ci)
            x1r = bu_ref[r0 + _SUBLANES:r0 + 2 * _SUBLANES, 0:_HALF] + cr
            x1i = bu_ref[r0 + _SUBLANES:r0 + 2 * _SUBLANES, _HALF:_STATE_W] + ci
            xb_ref[r0:r0 + 2 * _SUBLANES, 0:_HALF] = jnp.concatenate([x0r, x1r], axis=0).astype(_BF16)
            xb_ref[r0:r0 + 2 * _SUBLANES, _HALF:_STATE_W] = jnp.concatenate([x0i, x1i], axis=0).astype(_BF16)

    xb1_ref[...] = jnp.zeros_like(xb1_ref)
    project_in(0, bu0_ref)

    def pair_body(k, c):
        m0 = 2 * k
        project_in(m0 + 1, bu1_ref)
        project_out(jnp.maximum(m0 - 1, 0), xb1_ref)
        scan(m0, bu0_ref, xb0_ref)
        project_in(jnp.minimum(m0 + 2, n_tiles - 1), bu0_ref)
        project_out(m0, xb0_ref)
        scan(m0 + 1, bu1_ref, xb1_ref)
        return c

    lax.fori_loop(0, n_tiles // 2, pair_body, 0)
    project_out(n_tiles - 1, xb1_ref)

    y = jnp.concatenate([y_ref[m] for m in range(n_tiles)], axis=1)
    y_hi = y.astype(_BF16)
    y_lo = (y - y_hi.astype(_F32)).astype(_BF16)
    pt = pt_ref[...]
    yn = (jnp.dot(pt, y_hi, preferred_element_type=_F32)
          + jnp.dot(pt, y_lo, preferred_element_type=_F32))
    yn = yn + dsk_ref[...] * h_ref[...]
    cdf = 0.5 * (1.0 + jnp.tanh(math.sqrt(2.0 / math.pi) * (yn + 0.044715 * (yn * yn * yn))))
    z_ref[...] = (yn * cdf).astype(z_ref.dtype)


def _s5_tables(lam_re, lam_im, log_dt, b_re, b_im, c_re, c_im):
    g, p = lam_re.shape
    c = b_re.shape[-1]
    nm = g // _GROUPS_PER_TILE
    lr = lam_re.astype(_F32)
    li = lam_im.astype(_F32)
    dt = jnp.exp(log_dt.astype(_F32))[:, None]
    mag = jnp.exp(lr * dt)
    ang = li * dt
    lb_re = mag * jnp.cos(ang)
    lb_im = mag * jnp.sin(ang)
    nr = lb_re - 1.0
    den = lr * lr + li * li
    f_re = (nr * lr + lb_im * li) / den
    f_im = (lb_im * lr - nr * li) / den
    br = b_re.astype(_F32)
    bi = b_im.astype(_F32)
    bb_re = f_re[..., None] * br - f_im[..., None] * bi
    bb_im = f_re[..., None] * bi + f_im[..., None] * br
    eye = jnp.eye(_GROUPS_PER_TILE, dtype=_F32)

    def blk_b(bb):
        t = bb.reshape(nm, _GROUPS_PER_TILE, p, c)
        return jnp.einsum('mgpc,gh->mgchp', t, eye).reshape(nm, _GROUPS_PER_TILE * c, _GROUPS_PER_TILE * p)

    def blk_c(cc):
        t = cc.reshape(nm, _GROUPS_PER_TILE, c, p)
        return jnp.einsum('mgcp,gh->mgphc', t, eye).reshape(nm, _GROUPS_PER_TILE * p, _GROUPS_PER_TILE * c)

    wb = jnp.concatenate([blk_b(bb_re), blk_b(bb_im)], axis=-1).astype(_BF16)
    wc = jnp.concatenate([blk_c(c_re.astype(_F32)), blk_c(-c_im.astype(_F32))], axis=1).astype(_BF16)
    lam = jnp.concatenate([lb_re.reshape(nm, _HALF), lb_im.reshape(nm, _HALF)], axis=-1)
    lam8 = jnp.broadcast_to(lam[:, None, :], (nm, _SUBLANES, _STATE_W))
    return lam8, wb, wc


def _s5_mixer(x2, bsz, seq, gain, d_skip, lam8, wb, wc, *, tile=256):
    t_tok, d = x2.shape
    nm = lam8.shape[0]
    lc = tile // _SUBLANES
    nt = seq // tile
    r = jnp.arange(tile)
    col = (r % _SUBLANES) * lc + r // _SUBLANES
    perm = (col[:, None] == jnp.arange(tile)[None, :]).astype(_BF16)
    perm_t = perm.T
    const3 = lambda b, t: (0, 0, 0)
    const2 = lambda b, t: (0, 0)
    return pl.pallas_call(
        functools.partial(_s5_kernel, lc=lc),
        grid=(bsz, nt),
        in_specs=[
            pl.BlockSpec((tile, d), lambda b, t: (b * nt + t, 0)),
            pl.BlockSpec((1, d), const2),
            pl.BlockSpec((1, d), const2),
            pl.BlockSpec((nm, _SUBLANES, _STATE_W), const3),
            pl.BlockSpec((nm, _LANES, _STATE_W), const3),
            pl.BlockSpec((nm, _STATE_W, _LANES), const3),
            pl.BlockSpec((tile, tile), const2),
            pl.BlockSpec((tile, tile), const2),
        ],
        out_specs=pl.BlockSpec((tile, d), lambda b, t: (b * nt + t, 0)),
        out_shape=jax.ShapeDtypeStruct((t_tok, d), _BF16),
        scratch_shapes=[
            pltpu.VMEM((tile, d), _F32),
            pltpu.VMEM((nm, tile, _LANES), _BF16),
            pltpu.VMEM((nm, tile, _LANES), _F32),
            pltpu.VMEM((tile, _STATE_W), _F32),
            pltpu.VMEM((tile, _STATE_W), _F32),
            pltpu.VMEM((tile, _STATE_W), _BF16),
            pltpu.VMEM((tile, _STATE_W), _BF16),
            pltpu.VMEM((nm, _SUBLANES, _STATE_W), _F32),
            pltpu.VMEM((nm, _SUBLANES, _STATE_W), _F32),
            pltpu.VMEM((nm, 3 * _SUBLANES, _STATE_W), _F32),
        ],
        compiler_params=_params(("arbitrary", "arbitrary")),
        name="s5_mixer",
    )(x2, gain.reshape(1, d).astype(_F32), d_skip.reshape(1, d).astype(_F32), lam8, wb, wc, perm, perm_t)


def _glu_kernel(z_ref, wv_ref, wg_ref, x_ref, o_ref):
    z = z_ref[...]
    val = jnp.dot(z, wv_ref[...], preferred_element_type=_F32)
    gate = jnp.dot(z, wg_ref[...], preferred_element_type=_F32)
    o_ref[...] = x_ref[...] + val * _sigmoid(gate)


def _glu_residual(z, w_glu, layer, x2, *, tm=1024, tn=512):
    t_tok, d = x2.shape
    nj = d // tn
    return pl.pallas_call(
        _glu_kernel,
        grid=(t_tok // tm, nj),
        in_specs=[
            pl.BlockSpec((tm, d), lambda i, j: (i, 0)),
            pl.BlockSpec((None, d, tn), lambda i, j: (layer, 0, j)),
            pl.BlockSpec((None, d, tn), lambda i, j: (layer, 0, j + nj)),
            pl.BlockSpec((tm, tn), lambda i, j: (i, j)),
        ],
        out_specs=pl.BlockSpec((tm, tn), lambda i, j: (i, j)),
        out_shape=jax.ShapeDtypeStruct((t_tok, d), _F32),
        compiler_params=_params(("parallel", "arbitrary")),
        name="glu_residual",
    )(z, w_glu, w_glu, x2)


def _ffn_kernel(x_ref, g_ref, wg_ref, wu_ref, wo_ref, fg_ref, o_ref, hn_ref, *, final_norm, row_split):
    f = pl.program_id(1)
    rows = o_ref.shape[0] // row_split
    slabs = [slice(s * rows, (s + 1) * rows) for s in range(row_split)]

    @pl.when(f == 0)
    def _start():
        for sl in slabs:
            x = x_ref[sl, :]
            hn_ref[sl, :] = (x * _rms_scale(x) * g_ref[...]).astype(_BF16)
            o_ref[sl, :] = x

    for sl in slabs:
        hn = hn_ref[sl, :]
        a = jnp.dot(hn, wg_ref[...], preferred_element_type=_F32)
        u = jnp.dot(hn, wu_ref[...], preferred_element_type=_F32)
        act = (a * _sigmoid(a) * u).astype(_BF16)
        o_ref[sl, :] += jnp.dot(act, wo_ref[...], preferred_element_type=_F32)

    if final_norm:
        @pl.when(f == pl.num_programs(1) - 1)
        def _finish():
            for sl in slabs:
                y = o_ref[sl, :]
                o_ref[sl, :] = y * _rms_scale(y) * fg_ref[...]


def _ffn_residual(x2, gain, w_in, w_out, layer, final_gain=None, *, tm=1024, tf=512, row_split=2):
    t_tok, d = x2.shape
    hidden = w_out.shape[1]
    nf = hidden // tf
    final_norm = final_gain is not None
    fg = (final_gain if final_norm else gain).reshape(1, d).astype(_F32)
    return pl.pallas_call(
        functools.partial(_ffn_kernel, final_norm=final_norm, row_split=row_split),
        grid=(t_tok // tm, nf),
        in_specs=[
            pl.BlockSpec((tm, d), lambda i, f: (i, 0)),
            pl.BlockSpec((1, d), lambda i, f: (0, 0)),
            pl.BlockSpec((None, d, tf), lambda i, f: (layer, 0, f)),
            pl.BlockSpec((None, d, tf), lambda i, f: (layer, 0, f + nf)),
            pl.BlockSpec((None, tf, d), lambda i, f: (layer, f, 0)),
            pl.BlockSpec((1, d), lambda i, f: (0, 0)),
        ],
        out_specs=pl.BlockSpec((tm, d), lambda i, f: (i, 0)),
        out_shape=jax.ShapeDtypeStruct((t_tok, d), _F32),
        scratch_shapes=[pltpu.VMEM((tm, d), _BF16)],
        compiler_params=_params(("parallel", "arbitrary")),
        name="ffn_residual",
    )(x2, gain.reshape(1, d).astype(_F32), w_in, w_in, w_out, fg)


_PERM_ROWS = 256


def _perm_matrix(dil, rows=_PERM_ROWS):
    dst = jnp.arange(rows)
    src = (dst % (rows // dil)) * dil + dst // (rows // dil)
    return (src[:, None] == jnp.arange(rows)[None, :]).astype(_BF16)


def _to_residue_major(y, p_ref, dil, store):
    tm = y.shape[0]
    run = _PERM_ROWS // dil
    for j in range(tm // _PERM_ROWS):
        yp = jnp.dot(p_ref[...], y[j * _PERM_ROWS:(j + 1) * _PERM_ROWS], preferred_element_type=_F32).astype(_BF16)
        for r in range(dil):
            store(r, j * run, yp[r * run:(r + 1) * run])


def _qkv_kernel(*refs, dils, nj):
    n_groups = len(dils)
    n_p = sum(1 for dl in dils if dl > 1)
    x_ref, gq_ref, gkv_ref, w_ref = refs[:4]
    p_refs = refs[4:4 + n_p]
    q_refs = refs[4 + n_p:4 + n_p + n_groups]
    kv_refs = refs[4 + n_p + n_groups:4 + n_p + 2 * n_groups]
    hn_ref = refs[-1]
    tm = hn_ref.shape[1]
    p_of = {}
    for dl in dils:
        if dl > 1:
            p_of[dl] = p_refs[len(p_of)]
    j = pl.program_id(2)

    @pl.when(j == 0)
    def _start():
        for sb in range(tm // _PERM_ROWS):
            rows = slice(sb * _PERM_ROWS, (sb + 1) * _PERM_ROWS)
            x = x_ref[rows, :]
            xr = x * _rms_scale(x)
            hq = (xr * gq_ref[...]).astype(_BF16)
            hn_ref[n_groups, rows, :] = (xr * gkv_ref[...]).astype(_BF16)
            for g, dl in enumerate(dils):
                if dl == 1:
                    hn_ref[g, rows, :] = hq
                else:
                    run = _PERM_ROWS // dl
                    yp = jnp.dot(p_of[dl][...], hq, preferred_element_type=_F32).astype(_BF16)
                    for r in range(dl):
                        dst = r * (tm // dl) + sb * run
                        hn_ref[g, dst:dst + run, :] = yp[r * run:(r + 1) * run]

    for g, dl in enumerate(dils):
        @pl.when((j >= g * nj) & (j < (g + 1) * nj))
        def _q_tile(g=g, dl=dl):
            res = jnp.dot(hn_ref[g], w_ref[...], preferred_element_type=_F32).astype(_BF16)
            for r in range(dl):
                q_refs[g][r] = res[r * (tm // dl):(r + 1) * (tm // dl)]

    @pl.when(j >= n_groups * nj)
    def _kv_tile():
        kv = jnp.dot(hn_ref[n_groups], w_ref[...], preferred_element_type=_F32).astype(_BF16)
        for g, dl in enumerate(dils):
            if dl == 1:
                kv_refs[g][0] = kv
            else:
                def store(r, row, block, g=g):
                    kv_refs[g][r, row:row + block.shape[0], :] = block
                _to_residue_major(kv, p_of[dl], dl, store)


def _qkv_project(x2, bsz, seq, q_gain, kv_gain, w_cat, qw, dils, *, tm=1024, tn=512):
    t_tok, d = x2.shape
    n_groups = len(dils)
    kvw = w_cat.shape[1] - n_groups * qw
    assert qw % tn == 0 and kvw % tn == 0
    nj = qw // tn
    nkv = kvw // tn
    nt = seq // tm
    perms = [_perm_matrix(dl) for dl in dils if dl > 1]

    def q_spec(g, dl):
        return pl.BlockSpec((None, dl, tm // dl, tn),
                            lambda b, t, j: (b, 0, t, jnp.clip(j - g * nj, 0, nj - 1)))

    outs = pl.pallas_call(
        functools.partial(_qkv_kernel, dils=tuple(dils), nj=nj),
        grid=(bsz, nt, n_groups * nj + nkv),
        in_specs=([pl.BlockSpec((tm, d), lambda b, t, j: (b * nt + t, 0)),
                   pl.BlockSpec((1, d), lambda b, t, j: (0, 0)),
                   pl.BlockSpec((1, d), lambda b, t, j: (0, 0)),
                   pl.BlockSpec((d, tn), lambda b, t, j: (0, j))]
                  + [pl.BlockSpec((_PERM_ROWS, _PERM_ROWS), lambda b, t, j: (0, 0))] * len(perms)),
        out_specs=([q_spec(g, dl) for g, dl in enumerate(dils)]
                   + [pl.BlockSpec((None, dl, tm // dl, tn),
                                   lambda b, t, j: (b, 0, t, jnp.clip(j - n_groups * nj, 0, nkv - 1)))
                      for dl in dils]),
        out_shape=([jax.ShapeDtypeStruct((bsz, dl, seq // dl, qw), _BF16) for dl in dils]
                   + [jax.ShapeDtypeStruct((bsz, dl, seq // dl, kvw), _BF16) for dl in dils]),
        scratch_shapes=[pltpu.VMEM((n_groups + 1, tm, d), _BF16)],
        compiler_params=_params(("parallel", "parallel", "arbitrary")),
        name="qkv_project",
    )(x2, q_gain.reshape(1, d).astype(_F32), kv_gain.reshape(1, d).astype(_F32), w_cat, *perms)
    return outs[:n_groups], outs[n_groups:]


def _attn_kernel(q_ref, kp_ref, kc_ref, vp_ref, vc_ref, o_ref, st_ref, *, blk, rep, scale):
    i = pl.program_id(2)
    nq = q_ref.shape[0] // blk
    qi = lax.broadcasted_iota(jnp.int32, (blk, 2 * blk), 0)
    si = lax.broadcasted_iota(jnp.int32, (blk, 2 * blk), 1)
    dist = qi + blk - si
    band = (dist >= 0) & (dist <= blk)
    band_first = band & ((i > 0) | (si >= blk))
    lane = lax.broadcasted_iota(jnp.int32, (blk, _LANES), 1)
    n_kv = kc_ref.shape[1] // _HEAD_DIM
    for sub in range(nq):
        valid = jnp.concatenate([band_first if sub == 0 else band] * rep, axis=0)
        rows = slice(sub * blk, (sub + 1) * blk)
        st = jnp.zeros((blk, _LANES), _F32)
        for kvh in range(n_kv):
            ks = slice(kvh * _HEAD_DIM, (kvh + 1) * _HEAD_DIM)
            if sub == 0:
                k = jnp.concatenate([kp_ref[:, ks], kc_ref[0:blk, ks]], axis=0)
                v = jnp.concatenate([vp_ref[:, ks], vc_ref[0:blk, ks]], axis=0)
            else:
                k = kc_ref[(sub - 1) * blk:(sub + 1) * blk, ks]
                v = vc_ref[(sub - 1) * blk:(sub + 1) * blk, ks]
            q = jnp.concatenate(
                [q_ref[rows, (kvh * rep + rp) * _HEAD_DIM:(kvh * rep + rp + 1) * _HEAD_DIM] for rp in range(rep)],
                axis=0)
            s = lax.dot_general(q, k, (((1,), (1,)), ((), ())), preferred_element_type=_F32) * scale
            s = jnp.where(valid, s, _NEG_INF)
            m = jnp.max(s, axis=-1, keepdims=True)
            p = jnp.exp(s - m)
            l = jnp.sum(p, axis=-1, keepdims=True)
            pn = (p * (1.0 / l)).astype(_BF16)
            o = jnp.dot(pn, v, preferred_element_type=_F32)
            lse = m + jnp.log(l)
            for rp in range(rep):
                hd = kvh * rep + rp
                o_ref[rows, hd * _HEAD_DIM:(hd + 1) * _HEAD_DIM] = o[rp * blk:(rp + 1) * blk].astype(o_ref.dtype)
                st = jnp.where(lane == hd, lse[rp * blk:(rp + 1) * blk], st)
        st_ref[rows, :] = st


def _dilated_attention(q, kv, window, dilation, *, nq=2):
    bsz, dil, n, qw = q.shape
    assert dil == dilation
    kw = kv.shape[3] // 2
    n_heads = qw // _HEAD_DIM
    rep = n_heads // (kw // _HEAD_DIM)
    blk = window // dilation
    assert n % (nq * blk) == 0 and blk == _LANES and n_heads <= _LANES
    nb = n // (nq * blk)
    prev = lambda i: jnp.maximum(nq * i - 1, 0)
    return pl.pallas_call(
        functools.partial(_attn_kernel, blk=blk, rep=rep, scale=_HEAD_DIM ** -0.5),
        grid=(bsz, dilation, nb),
        in_specs=[
            pl.BlockSpec((None, None, nq * blk, qw), lambda b, r, i: (b, r, i, 0)),
            pl.BlockSpec((None, None, blk, kw), lambda b, r, i: (b, r, prev(i), 0)),
            pl.BlockSpec((None, None, nq * blk, kw), lambda b, r, i: (b, r, i, 0)),
            pl.BlockSpec((None, None, blk, kw), lambda b, r, i: (b, r, prev(i), 1)),
            pl.BlockSpec((None, None, nq * blk, kw), lambda b, r, i: (b, r, i, 1)),
        ],
        out_specs=[
            pl.BlockSpec((None, None, nq * blk, qw), lambda b, r, i: (b, r, i, 0)),
            pl.BlockSpec((None, None, nq * blk, _LANES), lambda b, r, i: (b, r, i, 0)),
        ],
        out_shape=[
            jax.ShapeDtypeStruct((bsz, dilation, n, qw), _BF16),
            jax.ShapeDtypeStruct((bsz, dilation, n, _LANES), _F32),
        ],
        compiler_params=_params(("parallel", "parallel", "arbitrary")),
        name=f"dilated_attention_d{dilation}",
    )(q, kv, kv, kv, kv)


def _combine_kernel(*refs, dils, n_heads):
    n_groups = len(dils)
    n_p = sum(1 for dl in dils if dl > 1)
    o_in = refs[:n_groups]
    s_in = refs[n_groups:2 * n_groups]
    x_ref, wo_ref = refs[2 * n_groups:2 * n_groups + 2]
    pt_refs = refs[2 * n_groups + 2:2 * n_groups + 2 + n_p]
    out_ref, onat_ref, snat_ref, comb_ref = refs[2 * n_groups + 2 + n_p:]
    tm = x_ref.shape[0]

    p_of = {}
    for dl in dils:
        if dl > 1:
            p_of[dl] = pt_refs[len(p_of)]

    for j in range(tm // _PERM_ROWS):
        rs = slice(j * _PERM_ROWS, (j + 1) * _PERM_ROWS)
        for g, dl in enumerate(dils):
            if dl == 1:
                onat_ref[g, rs, :] = o_in[g][0, rs, :]
                snat_ref[g, rs, :] = s_in[g][0, rs, :]
                continue
            run = _PERM_ROWS // dl
            pt = p_of[dl][...]
            oc = jnp.concatenate([o_in[g][r, j * run:(j + 1) * run, :] for r in range(dl)], axis=0)
            onat_ref[g, rs, :] = jnp.dot(pt, oc, preferred_element_type=_F32).astype(_BF16)
            sc = jnp.concatenate([s_in[g][r, j * run:(j + 1) * run, :] for r in range(dl)], axis=0)
            s_hi = sc.astype(_BF16)
            rem = sc - s_hi.astype(_F32)
            s_mid = rem.astype(_BF16)
            s_lo = (rem - s_mid.astype(_F32)).astype(_BF16)
            snat_ref[g, rs, :] = (jnp.dot(pt, s_hi, preferred_element_type=_F32)
                                  + jnp.dot(pt, s_mid, preferred_element_type=_F32)
                                  + jnp.dot(pt, s_lo, preferred_element_type=_F32))

        lses = [snat_ref[g, rs, :] for g in range(n_groups)]
        mx = lses[0]
        for l in lses[1:]:
            mx = jnp.maximum(mx, l)
        es = [jnp.exp(l - mx) for l in lses]
        den = es[0]
        for e in es[1:]:
            den = den + e
        inv = 1.0 / den
        wts = [e * inv for e in es]
        for hd in range(n_heads):
            cs = slice(hd * _HEAD_DIM, (hd + 1) * _HEAD_DIM)
            acc = wts[0][:, hd:hd + 1] * onat_ref[0, rs, cs].astype(_F32)
            for g in range(1, n_groups):
                acc = acc + wts[g][:, hd:hd + 1] * onat_ref[g, rs, cs].astype(_F32)
            comb_ref[rs, cs] = acc.astype(_BF16)
        out_ref[rs, :] = x_ref[rs, :] + jnp.dot(comb_ref[rs, :], wo_ref[...], preferred_element_type=_F32)


def _combine_project(outs, stats, x2, w_o, layer, dils, *, tm=512):
    t_tok, d = x2.shape
    bsz, _, _, qw = outs[0].shape
    seq = t_tok // bsz
    nt = seq // tm
    n_groups = len(outs)
    perms_t = [_perm_matrix(dl).T for dl in dils if dl > 1]
    grp = lambda b, t: (b, 0, t, 0)
    return pl.pallas_call(
        functools.partial(_combine_kernel, dils=tuple(dils), n_heads=qw // _HEAD_DIM),
        grid=(bsz, nt),
        in_specs=([pl.BlockSpec((None, dl, tm // dl, qw), grp) for dl in dils]
                  + [pl.BlockSpec((None, dl, tm // dl, _LANES), grp) for dl in dils]
                  + [pl.BlockSpec((tm, d), lambda b, t: (b * nt + t, 0)),
                     pl.BlockSpec((None, qw, d), lambda b, t: (layer, 0, 0))]
                  + [pl.BlockSpec((_PERM_ROWS, _PERM_ROWS), lambda b, t: (0, 0))] * len(perms_t)),
        out_specs=pl.BlockSpec((tm, d), lambda b, t: (b * nt + t, 0)),
        out_shape=jax.ShapeDtypeStruct((t_tok, d), _F32),
        scratch_shapes=[pltpu.VMEM((n_groups, tm, qw), _BF16),
                        pltpu.VMEM((n_groups, tm, _LANES), _F32),
                        pltpu.VMEM((tm, qw), _BF16)],
        compiler_params=_params(("parallel", "arbitrary")),
        name="combine_project",
    )(*outs, *stats, x2, w_o, *perms_t)


def kernel(x, s5_lam_re, s5_lam_im, s5_log_dt, s5_b_re, s5_b_im, s5_c_re, s5_c_im, s5_d, s5_w_glu, a_norm_mix, ffn_norm, ffn_w_in, ffn_w_out, b_norm_mix, attn_w_q, attn_w_o, kv_norm, w_kv, final_norm):
    bsz, seq, d = x.shape
    n_a = a_norm_mix.shape[0]
    n_b = b_norm_mix.shape[0]
    depth = n_a + n_b
    n_groups = len(_DILATED_PATTERNS)
    dils = [dl for _, dl in _DILATED_PATTERNS]
    x2 = x.reshape(bsz * seq, d).astype(_F32)
    w_glu_b = s5_w_glu.astype(_BF16)
    w_in_b = ffn_w_in.astype(_BF16)
    w_out_b = ffn_w_out.astype(_BF16)
    w_o_b = attn_w_o.astype(_BF16)
    assert n_b == 1, "the fused q/kv projection covers the single attention layer of this model"
    qw = attn_w_q.shape[2] // n_groups
    for layer in range(depth):
        if layer < n_a:
            i = layer
            lam8, wb, wc = _s5_tables(s5_lam_re[i], s5_lam_im[i], s5_log_dt[i], s5_b_re[i], s5_b_im[i],
                                      s5_c_re[i], s5_c_im[i])
            z = _s5_mixer(x2, bsz, seq, a_norm_mix[i], s5_d[i], lam8, wb, wc)
            x2 = _glu_residual(z, w_glu_b, i, x2)
        else:
            j = layer - n_a
            w_cat = jnp.concatenate([attn_w_q[j], w_kv], axis=1).astype(_BF16)
            qs, kvs = _qkv_project(x2, bsz, seq, b_norm_mix[j], kv_norm, w_cat, qw, dils)
            outs, stats = [], []
            for g, (window, dilation) in enumerate(_DILATED_PATTERNS):
                o, st = _dilated_attention(qs[g], kvs[g], window, dilation)
                outs.append(o)
                stats.append(st)
            x2 = _combine_project(outs, stats, x2, w_o_b, j, dils)
        final_gain = final_norm if layer == depth - 1 else None
        x2 = _ffn_residual(x2, ffn_norm[layer], w_in_b, w_out_b, layer, final_gain)
    return x2.reshape(bsz, seq, d).astype(x.dtype)
```

```python
import functools
import math

import jax
import jax.numpy as jnp
from jax import lax
from jax.experimental import pallas as pl
from jax.experimental.pallas import tpu as pltpu

_F32 = jnp.float32
_BF16 = jnp.bfloat16
_EPS = 1e-6
_NEG_INF = -1e30

_LANES = 128
_SUBLANES = 8
_VMEM_LIMIT_BYTES = 60 * 1024 * 1024

_S5_GROUP_CH = 16
_S5_STATE = 64
_HEAD_DIM = 128
_N_KV_HEADS = 4
_DILATED_PATTERNS = ((128, 1), (512, 4), (2048, 16))
_GROUPS_PER_TILE = _LANES // _S5_GROUP_CH
_HALF = _GROUPS_PER_TILE * _S5_STATE
_STATE_W = 2 * _HALF


def _params(sem):
    return pltpu.CompilerParams(dimension_semantics=sem, vmem_limit_bytes=_VMEM_LIMIT_BYTES)


def _rms_scale(x):
    return lax.rsqrt(jnp.mean(x * x, axis=-1, keepdims=True) + _EPS)


def _sigmoid(v):
    return 1.0 / (1.0 + jnp.exp(-v))


def _cmul(ar, ai, br, bi):
    return ar * br - ai * bi, ar * bi + ai * br


def _s5_kernel(x_ref, g_ref, dsk_ref, lam_ref, wb_ref, wc_ref, p_ref, pt_ref, wglu_ref, o_ref,
               h_ref, hbp_ref, y_ref, bu0_ref, bu1_ref, xb0_ref, xb1_ref, carry_ref, cp_ref, lvl_ref,
               zprev_ref, xprev_ref, og_ref, *, lc, tiles_per_seq):
    s = pl.program_id(0)
    n_tiles = lam_ref.shape[0]
    n_glu = og_ref.shape[0]
    glu_w = og_ref.shape[2]
    rows = lax.broadcasted_iota(jnp.int32, (_SUBLANES, _HALF), 0)

    @pl.when(s % tiles_per_seq == 0)
    def _new_sequence():
        carry_ref[...] = jnp.zeros_like(carry_ref)

    @pl.when(s == 0)
    def _init():
        zprev_ref[...] = jnp.zeros_like(zprev_ref)
        xprev_ref[...] = jnp.zeros_like(xprev_ref)

        def init_m(m, c):
            pr = lam_ref[m, :, 0:_HALF]
            pi = lam_ref[m, :, _HALF:_STATE_W]
            for _ in range(int(math.log2(lc))):
                pr, pi = _cmul(pr, pi, pr, pi)
            qr, qi = pr, pi
            for lv in range(3):
                lvl_ref[m, lv * _SUBLANES:(lv + 1) * _SUBLANES, 0:_HALF] = qr
                lvl_ref[m, lv * _SUBLANES:(lv + 1) * _SUBLANES, _HALF:_STATE_W] = qi
                qr, qi = _cmul(qr, qi, qr, qi)
            cr, ci = pr, pi
            outr, outi = pr, pi
            for c_idx in range(1, _SUBLANES):
                cr, ci = _cmul(cr, ci, pr, pi)
                outr = jnp.where(rows == c_idx, cr, outr)
                outi = jnp.where(rows == c_idx, ci, outi)
            cp_ref[m, :, 0:_HALF] = outr
            cp_ref[m, :, _HALF:_STATE_W] = outi
            return c

        lax.fori_loop(0, n_tiles, init_m, 0)

    x = x_ref[...]
    h = x * _rms_scale(x) * g_ref[...]
    h_ref[...] = h
    hp = jnp.dot(p_ref[...], h.astype(_BF16), preferred_element_type=_F32)
    for m in range(n_tiles):
        hbp_ref[m] = hp[:, m * _LANES:(m + 1) * _LANES].astype(_BF16)

    def project_in(m, bu_ref):
        bu_ref[...] = jnp.dot(hbp_ref[m], wb_ref[m], preferred_element_type=_F32)

    def project_out(m, xb_ref):
        y_ref[m] = jnp.dot(xb_ref[...], wc_ref[m], preferred_element_type=_F32)

    def scan(m, bu_ref, xb_ref):
        lr = lam_ref[m, :, 0:_HALF]
        li = lam_ref[m, :, _HALF:_STATE_W]

        xr = jnp.zeros((_SUBLANES, _HALF), _F32)
        xi = jnp.zeros((_SUBLANES, _HALF), _F32)
        for tau in range(lc):
            r0 = tau * _SUBLANES
            tr, ti = _cmul(lr, li, xr, xi)
            xr = tr + bu_ref[r0:r0 + _SUBLANES, 0:_HALF]
            xi = ti + bu_ref[r0:r0 + _SUBLANES, _HALF:_STATE_W]
            bu_ref[r0:r0 + _SUBLANES, 0:_HALF] = xr
            bu_ref[r0:r0 + _SUBLANES, _HALF:_STATE_W] = xi

        zr, zi = xr, xi
        for lv, s in enumerate((1, 2, 4)):
            ar = lvl_ref[m, lv * _SUBLANES:(lv + 1) * _SUBLANES, 0:_HALF]
            ai = lvl_ref[m, lv * _SUBLANES:(lv + 1) * _SUBLANES, _HALF:_STATE_W]
            sr = jnp.where(rows >= s, pltpu.roll(zr, s, 0), 0.0)
            si = jnp.where(rows >= s, pltpu.roll(zi, s, 0), 0.0)
            tr, ti = _cmul(ar, ai, sr, si)
            zr = zr + tr
            zi = zi + ti
        pr = carry_ref[m, :, 0:_HALF]
        pi = carry_ref[m, :, _HALF:_STATE_W]
        tr, ti = _cmul(cp_ref[m, :, 0:_HALF], cp_ref[m, :, _HALF:_STATE_W], pr, pi)
        er = zr + tr
        ei = zi + ti
        cr = jnp.where(rows >= 1, pltpu.roll(er, 1, 0), pr)
        ci = jnp.where(rows >= 1, pltpu.roll(ei, 1, 0), pi)
        carry_ref[m, :, 0:_HALF] = jnp.broadcast_to(er[_SUBLANES - 1:_SUBLANES, :], (_SUBLANES, _HALF))
        carry_ref[m, :, _HALF:_STATE_W] = jnp.broadcast_to(ei[_SUBLANES - 1:_SUBLANES, :], (_SUBLANES, _HALF))

        for tau in range(0, lc, 2):
            r0 = tau * _SUBLANES
            cr, ci = _cmul(lr, li, cr, ci)
            x0r = bu_ref[r0:r0 + _SUBLANES, 0:_HALF] + cr
            x0i = bu_ref[r0:r0 + _SUBLANES, _HALF:_STATE_W] + ci
            cr, ci = _cmul(lr, li, cr, ci)
            x1r = bu_ref[r0 + _SUBLANES:r0 + 2 * _SUBLANES, 0:_HALF] + cr
            x1i = bu_ref[r0 + _SUBLANES:r0 + 2 * _SUBLANES, _HALF:_STATE_W] + ci
            xb_ref[r0:r0 + 2 * _SUBLANES, 0:_HALF] = jnp.concatenate([x0r, x1r], axis=0).astype(_BF16)
            xb_ref[r0:r0 + 2 * _SUBLANES, _HALF:_STATE_W] = jnp.concatenate([x0i, x1i], axis=0).astype(_BF16)

    xb1_ref[...] = jnp.zeros_like(xb1_ref)
    project_in(0, bu0_ref)

    def glu_value(k):
        og_ref[k] = jnp.dot(zprev_ref[...], wglu_ref[k], preferred_element_type=_F32)

    def glu_gate(k):
        gate = jnp.dot(zprev_ref[...], wglu_ref[k + n_glu], preferred_element_type=_F32)
        og_ref[k] = xprev_ref[k] + og_ref[k] * _sigmoid(gate)

    def pair_body(k, c):
        m0 = 2 * k
        project_in(m0 + 1, bu1_ref)
        project_out(jnp.maximum(m0 - 1, 0), xb1_ref)
        glu_value(k)
        scan(m0, bu0_ref, xb0_ref)
        project_in(jnp.minimum(m0 + 2, n_tiles - 1), bu0_ref)
        project_out(m0, xb0_ref)
        glu_gate(k)
        scan(m0 + 1, bu1_ref, xb1_ref)
        return c

    lax.fori_loop(0, n_tiles // 2, pair_body, 0)
    project_out(n_tiles - 1, xb1_ref)

    for k in range(n_glu):
        o_ref[:, k * glu_w:(k + 1) * glu_w] = og_ref[k]

    y = jnp.concatenate([y_ref[m] for m in range(n_tiles)], axis=1)
    y_hi = y.astype(_BF16)
    y_lo = (y - y_hi.astype(_F32)).astype(_BF16)
    pt = pt_ref[...]
    yn = (jnp.dot(pt, y_hi, preferred_element_type=_F32)
          + jnp.dot(pt, y_lo, preferred_element_type=_F32))
    yn = yn + dsk_ref[...] * h_ref[...]
    cdf = 0.5 * (1.0 + jnp.tanh(math.sqrt(2.0 / math.pi) * (yn + 0.044715 * (yn * yn * yn))))
    zprev_ref[...] = (yn * cdf).astype(zprev_ref.dtype)
    for k in range(n_glu):
        xprev_ref[k] = x_ref[:, k * glu_w:(k + 1) * glu_w]


def _s5_tables(lam_re, lam_im, log_dt, b_re, b_im, c_re, c_im):
    g, p = lam_re.shape
    c = b_re.shape[-1]
    nm = g // _GROUPS_PER_TILE
    lr = lam_re.astype(_F32)
    li = lam_im.astype(_F32)
    dt = jnp.exp(log_dt.astype(_F32))[:, None]
    mag = jnp.exp(lr * dt)
    ang = li * dt
    lb_re = mag * jnp.cos(ang)
    lb_im = mag * jnp.sin(ang)
    nr = lb_re - 1.0
    den = lr * lr + li * li
    f_re = (nr * lr + lb_im * li) / den
    f_im = (lb_im * lr - nr * li) / den
    br = b_re.astype(_F32)
    bi = b_im.astype(_F32)
    bb_re = f_re[..., None] * br - f_im[..., None] * bi
    bb_im = f_re[..., None] * bi + f_im[..., None] * br
    eye = jnp.eye(_GROUPS_PER_TILE, dtype=_F32)

    def blk_b(bb):
        t = bb.reshape(nm, _GROUPS_PER_TILE, p, c)
        return jnp.einsum('mgpc,gh->mgchp', t, eye).reshape(nm, _GROUPS_PER_TILE * c, _GROUPS_PER_TILE * p)

    def blk_c(cc):
        t = cc.reshape(nm, _GROUPS_PER_TILE, c, p)
        return jnp.einsum('mgcp,gh->mgphc', t, eye).reshape(nm, _GROUPS_PER_TILE * p, _GROUPS_PER_TILE * c)

    wb = jnp.concatenate([blk_b(bb_re), blk_b(bb_im)], axis=-1).astype(_BF16)
    wc = jnp.concatenate([blk_c(c_re.astype(_F32)), blk_c(-c_im.astype(_F32))], axis=1).astype(_BF16)
    lam = jnp.concatenate([lb_re.reshape(nm, _HALF), lb_im.reshape(nm, _HALF)], axis=-1)
    lam8 = jnp.broadcast_to(lam[:, None, :], (nm, _SUBLANES, _STATE_W))
    return lam8, wb, wc


def _s5_glu_layer(x2, bsz, seq, gain, d_skip, lam8, wb, wc, w_glu, *, tile=256):
    t_tok, d = x2.shape
    nm = lam8.shape[0]
    lc = tile // _SUBLANES
    nt = seq // tile
    n_steps = bsz * nt
    n_glu = nm // 2
    glu_w = d // n_glu
    assert glu_w % _LANES == 0
    r = jnp.arange(tile)
    col = (r % _SUBLANES) * lc + r // _SUBLANES
    perm = (col[:, None] == jnp.arange(tile)[None, :]).astype(_BF16)
    perm_t = perm.T
    w_chunks = w_glu.reshape(d, 2 * n_glu, glu_w).transpose(1, 0, 2)
    const3 = lambda s: (0, 0, 0)
    const2 = lambda s: (0, 0)
    once = dict(pipeline_mode=pl.Buffered(1))
    return pl.pallas_call(
        functools.partial(_s5_kernel, lc=lc, tiles_per_seq=nt),
        grid=(n_steps + 1,),
        in_specs=[
            pl.BlockSpec((tile, d), lambda s: (jnp.minimum(s, n_steps - 1), 0)),
            pl.BlockSpec((1, d), const2),
            pl.BlockSpec((1, d), const2),
            pl.BlockSpec((nm, _SUBLANES, _STATE_W), const3, **once),
            pl.BlockSpec((nm, _LANES, _STATE_W), const3, **once),
            pl.BlockSpec((nm, _STATE_W, _LANES), const3, **once),
            pl.BlockSpec((tile, tile), const2),
            pl.BlockSpec((tile, tile), const2),
            pl.BlockSpec((2 * n_glu, d, glu_w), const3, **once),
        ],
        out_specs=pl.BlockSpec((tile, d), lambda s: (jnp.maximum(s - 1, 0), 0)),
        out_shape=jax.ShapeDtypeStruct((t_tok, d), _F32),
        scratch_shapes=[
            pltpu.VMEM((tile, d), _F32),
            pltpu.VMEM((nm, tile, _LANES), _BF16),
            pltpu.VMEM((nm, tile, _LANES), _F32),
            pltpu.VMEM((tile, _STATE_W), _F32),
            pltpu.VMEM((tile, _STATE_W), _F32),
            pltpu.VMEM((tile, _STATE_W), _BF16),
            pltpu.VMEM((tile, _STATE_W), _BF16),
            pltpu.VMEM((nm, _SUBLANES, _STATE_W), _F32),
            pltpu.VMEM((nm, _SUBLANES, _STATE_W), _F32),
            pltpu.VMEM((nm, 3 * _SUBLANES, _STATE_W), _F32),
            pltpu.VMEM((tile, d), _BF16),
            pltpu.VMEM((n_glu, tile, glu_w), _F32),
            pltpu.VMEM((n_glu, tile, glu_w), _F32),
        ],
        compiler_params=_params(("arbitrary",)),
        name="s5_glu_layer",
    )(x2, gain.reshape(1, d).astype(_F32), d_skip.reshape(1, d).astype(_F32), lam8, wb, wc, perm, perm_t, w_chunks)


def _ffn_kernel(x_ref, g_ref, wg_ref, wu_ref, wo_ref, fg_ref, o_ref, hn_ref, *, final_norm, row_split):
    f = pl.program_id(1)
    rows = o_ref.shape[0] // row_split
    slabs = [slice(s * rows, (s + 1) * rows) for s in range(row_split)]

    @pl.when(f == 0)
    def _start():
        for sl in slabs:
            x = x_ref[sl, :]
            hn_ref[sl, :] = (x * _rms_scale(x) * g_ref[...]).astype(_BF16)
            o_ref[sl, :] = x

    for sl in slabs:
        hn = hn_ref[sl, :]
        a = jnp.dot(hn, wg_ref[...], preferred_element_type=_F32)
        u = jnp.dot(hn, wu_ref[...], preferred_element_type=_F32)
        act = (a * _sigmoid(a) * u).astype(_BF16)
        o_ref[sl, :] += jnp.dot(act, wo_ref[...], preferred_element_type=_F32)

    if final_norm:
        @pl.when(f == pl.num_programs(1) - 1)
        def _finish():
            for sl in slabs:
                y = o_ref[sl, :]
                o_ref[sl, :] = y * _rms_scale(y) * fg_ref[...]


def _ffn_residual(x2, gain, w_in, w_out, layer, final_gain=None, *, tm=1024, tf=512, row_split=2):
    t_tok, d = x2.shape
    hidden = w_out.shape[1]
    nf = hidden // tf
    final_norm = final_gain is not None
    fg = (final_gain if final_norm else gain).reshape(1, d).astype(_F32)
    return pl.pallas_call(
        functools.partial(_ffn_kernel, final_norm=final_norm, row_split=row_split),
        grid=(t_tok // tm, nf),
        in_specs=[
            pl.BlockSpec((tm, d), lambda i, f: (i, 0)),
            pl.BlockSpec((1, d), lambda i, f: (0, 0)),
            pl.BlockSpec((None, d, tf), lambda i, f: (layer, 0, f)),
            pl.BlockSpec((None, d, tf), lambda i, f: (layer, 0, f + nf)),
            pl.BlockSpec((None, tf, d), lambda i, f: (layer, f, 0)),
            pl.BlockSpec((1, d), lambda i, f: (0, 0)),
        ],
        out_specs=pl.BlockSpec((tm, d), lambda i, f: (i, 0)),
        out_shape=jax.ShapeDtypeStruct((t_tok, d), _F32),
        scratch_shapes=[pltpu.VMEM((tm, d), _BF16)],
        compiler_params=_params(("parallel", "arbitrary")),
        name="ffn_residual",
    )(x2, gain.reshape(1, d).astype(_F32), w_in, w_in, w_out, fg)


_PERM_ROWS = 256


def _perm_matrix(dil, rows=_PERM_ROWS):
    dst = jnp.arange(rows)
    src = (dst % (rows // dil)) * dil + dst // (rows // dil)
    return (src[:, None] == jnp.arange(rows)[None, :]).astype(_BF16)


def _to_residue_major(y, p_ref, dil, store):
    tm = y.shape[0]
    run = _PERM_ROWS // dil
    for j in range(tm // _PERM_ROWS):
        yp = jnp.dot(p_ref[...], y[j * _PERM_ROWS:(j + 1) * _PERM_ROWS], preferred_element_type=_F32).astype(_BF16)
        for r in range(dil):
            store(r, j * run, yp[r * run:(r + 1) * run])


def _qkv_kernel(*refs, dils, nj):
    n_groups = len(dils)
    n_p = sum(1 for dl in dils if dl > 1)
    x_ref, gq_ref, gkv_ref, w_ref = refs[:4]
    p_refs = refs[4:4 + n_p]
    q_refs = refs[4 + n_p:4 + n_p + n_groups]
    kv_refs = refs[4 + n_p + n_groups:4 + n_p + 2 * n_groups]
    hn_ref = refs[-1]
    tm = hn_ref.shape[1]
    p_of = {}
    for dl in dils:
        if dl > 1:
            p_of[dl] = p_refs[len(p_of)]
    j = pl.program_id(2)

    @pl.when(j == 0)
    def _start():
        for sb in range(tm // _PERM_ROWS):
            rows = slice(sb * _PERM_ROWS, (sb + 1) * _PERM_ROWS)
            x = x_ref[rows, :]
            xr = x * _rms_scale(x)
            hq = (xr * gq_ref[...]).astype(_BF16)
            hn_ref[n_groups, rows, :] = (xr * gkv_ref[...]).astype(_BF16)
            for g, dl in enumerate(dils):
                if dl == 1:
                    hn_ref[g, rows, :] = hq
                else:
                    run = _PERM_ROWS // dl
                    yp = jnp.dot(p_of[dl][...], hq, preferred_element_type=_F32).astype(_BF16)
                    for r in range(dl):
                        dst = r * (tm // dl) + sb * run
                        hn_ref[g, dst:dst + run, :] = yp[r * run:(r + 1) * run]

    for g, dl in enumerate(dils):
        @pl.when((j >= g * nj) & (j < (g + 1) * nj))
        def _q_tile(g=g, dl=dl):
            res = jnp.dot(hn_ref[g], w_ref[...], preferred_element_type=_F32).astype(_BF16)
            for r in range(dl):
                q_refs[g][r] = res[r * (tm // dl):(r + 1) * (tm // dl)]

    @pl.when(j >= n_groups * nj)
    def _kv_tile():
        kv = jnp.dot(hn_ref[n_groups], w_ref[...], preferred_element_type=_F32).astype(_BF16)
        for g, dl in enumerate(dils):
            if dl == 1:
                kv_refs[g][0] = kv
            else:
                def store(r, row, block, g=g):
                    kv_refs[g][r, row:row + block.shape[0], :] = block
                _to_residue_major(kv, p_of[dl], dl, store)


def _qkv_project(x2, bsz, seq, q_gain, kv_gain, w_cat, qw, dils, *, tm=1024, tn=512):
    t_tok, d = x2.shape
    n_groups = len(dils)
    kvw = w_cat.shape[1] - n_groups * qw
    assert qw % tn == 0 and kvw % tn == 0
    nj = qw // tn
    nkv = kvw // tn
    nt = seq // tm
    perms = [_perm_matrix(dl) for dl in dils if dl > 1]

    def q_spec(g, dl):
        return pl.BlockSpec((None, dl, tm // dl, tn),
                            lambda b, t, j: (b, 0, t, jnp.clip(j - g * nj, 0, nj - 1)))

    outs = pl.pallas_call(
        functools.partial(_qkv_kernel, dils=tuple(dils), nj=nj),
        grid=(bsz, nt, n_groups * nj + nkv),
        in_specs=([pl.BlockSpec((tm, d), lambda b, t, j: (b * nt + t, 0)),
                   pl.BlockSpec((1, d), lambda b, t, j: (0, 0)),
                   pl.BlockSpec((1, d), lambda b, t, j: (0, 0)),
                   pl.BlockSpec((d, tn), lambda b, t, j: (0, j))]
                  + [pl.BlockSpec((_PERM_ROWS, _PERM_ROWS), lambda b, t, j: (0, 0))] * len(perms)),
        out_specs=([q_spec(g, dl) for g, dl in enumerate(dils)]
                   + [pl.BlockSpec((None, dl, tm // dl, tn),
                                   lambda b, t, j: (b, 0, t, jnp.clip(j - n_groups * nj, 0, nkv - 1)))
                      for dl in dils]),
        out_shape=([jax.ShapeDtypeStruct((bsz, dl, seq // dl, qw), _BF16) for dl in dils]
                   + [jax.ShapeDtypeStruct((bsz, dl, seq // dl, kvw), _BF16) for dl in dils]),
        scratch_shapes=[pltpu.VMEM((n_groups + 1, tm, d), _BF16)],
        compiler_params=_params(("parallel", "parallel", "arbitrary")),
        name="qkv_project",
    )(x2, q_gain.reshape(1, d).astype(_F32), kv_gain.reshape(1, d).astype(_F32), w_cat, *perms)
    return outs[:n_groups], outs[n_groups:]


def _attn_kernel(q_ref, kp_ref, kc_ref, vp_ref, vc_ref, o_ref, st_ref, *, blk, rep, scale):
    i = pl.program_id(2)
    nq = q_ref.shape[0] // blk
    qi = lax.broadcasted_iota(jnp.int32, (blk, 2 * blk), 0)
    si = lax.broadcasted_iota(jnp.int32, (blk, 2 * blk), 1)
    dist = qi + blk - si
    band = (dist >= 0) & (dist <= blk)
    band_first = band & ((i > 0) | (si >= blk))
    lane = lax.broadcasted_iota(jnp.int32, (blk, _LANES), 1)
    n_kv = kc_ref.shape[1] // _HEAD_DIM
    for sub in range(nq):
        valid = jnp.concatenate([band_first if sub == 0 else band] * rep, axis=0)
        rows = slice(sub * blk, (sub + 1) * blk)
        st = jnp.zeros((blk, _LANES), _F32)
        for kvh in range(n_kv):
            ks = slice(kvh * _HEAD_DIM, (kvh + 1) * _HEAD_DIM)
            if sub == 0:
                k = jnp.concatenate([kp_ref[:, ks], kc_ref[0:blk, ks]], axis=0)
                v = jnp.concatenate([vp_ref[:, ks], vc_ref[0:blk, ks]], axis=0)
            else:
                k = kc_ref[(sub - 1) * blk:(sub + 1) * blk, ks]
                v = vc_ref[(sub - 1) * blk:(sub + 1) * blk, ks]
            q = jnp.concatenate(
                [q_ref[rows, (kvh * rep + rp) * _HEAD_DIM:(kvh * rep + rp + 1) * _HEAD_DIM] for rp in range(rep)],
                axis=0)
            s = lax.dot_general(q, k, (((1,), (1,)), ((), ())), preferred_element_type=_F32) * scale
            s = jnp.where(valid, s, _NEG_INF)
            m = jnp.max(s, axis=-1, keepdims=True)
            p = jnp.exp(s - m)
            l = jnp.sum(p, axis=-1, keepdims=True)
            pn = (p * (1.0 / l)).astype(_BF16)
            o = jnp.dot(pn, v, preferred_element_type=_F32)
            lse = m + jnp.log(l)
            for rp in range(rep):
                hd = kvh * rep + rp
                o_ref[rows, hd * _HEAD_DIM:(hd + 1) * _HEAD_DIM] = o[rp * blk:(rp + 1) * blk].astype(o_ref.dtype)
                st = jnp.where(lane == hd, lse[rp * blk:(rp + 1) * blk], st)
        st_ref[rows, :] = st


def _dilated_attention(q, kv, window, dilation, *, nq=2):
    bsz, dil, n, qw = q.shape
    assert dil == dilation
    kw = kv.shape[3] // 2
    n_heads = qw // _HEAD_DIM
    rep = n_heads // (kw // _HEAD_DIM)
    blk = window // dilation
    assert n % (nq * blk) == 0 and blk == _LANES and n_heads <= _LANES
    nb = n // (nq * blk)
    prev = lambda i: jnp.maximum(nq * i - 1, 0)
    return pl.pallas_call(
        functools.partial(_attn_kernel, blk=blk, rep=rep, scale=_HEAD_DIM ** -0.5),
        grid=(bsz, dilation, nb),
        in_specs=[
            pl.BlockSpec((None, None, nq * blk, qw), lambda b, r, i: (b, r, i, 0)),
            pl.BlockSpec((None, None, blk, kw), lambda b, r, i: (b, r, prev(i), 0)),
            pl.BlockSpec((None, None, nq * blk, kw), lambda b, r, i: (b, r, i, 0)),
            pl.BlockSpec((None, None, blk, kw), lambda b, r, i: (b, r, prev(i), 1)),
            pl.BlockSpec((None, None, nq * blk, kw), lambda b, r, i: (b, r, i, 1)),
        ],
        out_specs=[
            pl.BlockSpec((None, None, nq * blk, qw), lambda b, r, i: (b, r, i, 0)),
            pl.BlockSpec((None, None, nq * blk, _LANES), lambda b, r, i: (b, r, i, 0)),
        ],
        out_shape=[
            jax.ShapeDtypeStruct((bsz, dilation, n, qw), _BF16),
            jax.ShapeDtypeStruct((bsz, dilation, n, _LANES), _F32),
        ],
        compiler_params=_params(("parallel", "parallel", "arbitrary")),
        name=f"dilated_attention_d{dilation}",
    )(q, kv, kv, kv, kv)


def _combine_kernel(*refs, dils, n_heads):
    n_groups = len(dils)
    n_p = sum(1 for dl in dils if dl > 1)
    o_in = refs[:n_groups]
    s_in = refs[n_groups:2 * n_groups]
    x_ref, wo_ref = refs[2 * n_groups:2 * n_groups + 2]
    pt_refs = refs[2 * n_groups + 2:2 * n_groups + 2 + n_p]
    out_ref, onat_ref, snat_ref, comb_ref = refs[2 * n_groups + 2 + n_p:]
    tm = x_ref.shape[0]

    p_of = {}
    for dl in dils:
        if dl > 1:
            p_of[dl] = pt_refs[len(p_of)]

    for j in range(tm // _PERM_ROWS):
        rs = slice(j * _PERM_ROWS, (j + 1) * _PERM_ROWS)
        for g, dl in enumerate(dils):
            if dl == 1:
                onat_ref[g, rs, :] = o_in[g][0, rs, :]
                snat_ref[g, rs, :] = s_in[g][0, rs, :]
                continue
            run = _PERM_ROWS // dl
            pt = p_of[dl][...]
            oc = jnp.concatenate([o_in[g][r, j * run:(j + 1) * run, :] for r in range(dl)], axis=0)
            onat_ref[g, rs, :] = jnp.dot(pt, oc, preferred_element_type=_F32).astype(_BF16)
            sc = jnp.concatenate([s_in[g][r, j * run:(j + 1) * run, :] for r in range(dl)], axis=0)
            s_hi = sc.astype(_BF16)
            rem = sc - s_hi.astype(_F32)
            s_mid = rem.astype(_BF16)
            s_lo = (rem - s_mid.astype(_F32)).astype(_BF16)
            snat_ref[g, rs, :] = (jnp.dot(pt, s_hi, preferred_element_type=_F32)
                                  + jnp.dot(pt, s_mid, preferred_element_type=_F32)
                                  + jnp.dot(pt, s_lo, preferred_element_type=_F32))

        lses = [snat_ref[g, rs, :] for g in range(n_groups)]
        mx = lses[0]
        for l in lses[1:]:
            mx = jnp.maximum(mx, l)
        es = [jnp.exp(l - mx) for l in lses]
        den = es[0]
        for e in es[1:]:
            den = den + e
        inv = 1.0 / den
        wts = [e * inv for e in es]
        for hd in range(n_heads):
            cs = slice(hd * _HEAD_DIM, (hd + 1) * _HEAD_DIM)
            acc = wts[0][:, hd:hd + 1] * onat_ref[0, rs, cs].astype(_F32)
            for g in range(1, n_groups):
                acc = acc + wts[g][:, hd:hd + 1] * onat_ref[g, rs, cs].astype(_F32)
            comb_ref[rs, cs] = acc.astype(_BF16)
        out_ref[rs, :] = x_ref[rs, :] + jnp.dot(comb_ref[rs, :], wo_ref[...], preferred_element_type=_F32)


def _combine_project(outs, stats, x2, w_o, layer, dils, *, tm=512):
    t_tok, d = x2.shape
    bsz, _, _, qw = outs[0].shape
    seq = t_tok // bsz
    nt = seq // tm
    n_groups = len(outs)
    perms_t = [_perm_matrix(dl).T for dl in dils if dl > 1]
    grp = lambda b, t: (b, 0, t, 0)
    return pl.pallas_call(
        functools.partial(_combine_kernel, dils=tuple(dils), n_heads=qw // _HEAD_DIM),
        grid=(bsz, nt),
        in_specs=([pl.BlockSpec((None, dl, tm // dl, qw), grp) for dl in dils]
                  + [pl.BlockSpec((None, dl, tm // dl, _LANES), grp) for dl in dils]
                  + [pl.BlockSpec((tm, d), lambda b, t: (b * nt + t, 0)),
                     pl.BlockSpec((None, qw, d), lambda b, t: (layer, 0, 0))]
                  + [pl.BlockSpec((_PERM_ROWS, _PERM_ROWS), lambda b, t: (0, 0))] * len(perms_t)),
        out_specs=pl.BlockSpec((tm, d), lambda b, t: (b * nt + t, 0)),
        out_shape=jax.ShapeDtypeStruct((t_tok, d), _F32),
        scratch_shapes=[pltpu.VMEM((n_groups, tm, qw), _BF16),
                        pltpu.VMEM((n_groups, tm, _LANES), _F32),
                        pltpu.VMEM((tm, qw), _BF16)],
        compiler_params=_params(("parallel", "arbitrary")),
        name="combine_project",
    )(*outs, *stats, x2, w_o, *perms_t)


def kernel(x, s5_lam_re, s5_lam_im, s5_log_dt, s5_b_re, s5_b_im, s5_c_re, s5_c_im, s5_d, s5_w_glu, a_norm_mix, ffn_norm, ffn_w_in, ffn_w_out, b_norm_mix, attn_w_q, attn_w_o, kv_norm, w_kv, final_norm):
    bsz, seq, d = x.shape
    n_a = a_norm_mix.shape[0]
    n_b = b_norm_mix.shape[0]
    depth = n_a + n_b
    n_groups = len(_DILATED_PATTERNS)
    dils = [dl for _, dl in _DILATED_PATTERNS]
    x2 = x.reshape(bsz * seq, d).astype(_F32)
    w_in_b = ffn_w_in.astype(_BF16)
    w_out_b = ffn_w_out.astype(_BF16)
    w_o_b = attn_w_o.astype(_BF16)
    assert n_b == 1, "the fused q/kv projection covers the single attention layer of this model"
    qw = attn_w_q.shape[2] // n_groups
    for layer in range(depth):
        if layer < n_a:
            i = layer
            lam8, wb, wc = _s5_tables(s5_lam_re[i], s5_lam_im[i], s5_log_dt[i], s5_b_re[i], s5_b_im[i],
                                      s5_c_re[i], s5_c_im[i])
            x2 = _s5_glu_layer(x2, bsz, seq, a_norm_mix[i], s5_d[i], lam8, wb, wc, s5_w_glu[i].astype(_BF16))
        else:
            j = layer - n_a
            w_cat = jnp.concatenate([attn_w_q[j], w_kv], axis=1).astype(_BF16)
            qs, kvs = _qkv_project(x2, bsz, seq, b_norm_mix[j], kv_norm, w_cat, qw, dils)
            outs, stats = [], []
            for g, (window, dilation) in enumerate(_DILATED_PATTERNS):
                o, st = _dilated_attention(qs[g], kvs[g], window, dilation)
                outs.append(o)
                stats.append(st)
            x2 = _combine_project(outs, stats, x2, w_o_b, j, dils)
        final_gain = final_norm if layer == depth - 1 else None
        x2 = _ffn_residual(x2, ffn_norm[layer], w_in_b, w_out_b, layer, final_gain)
    return x2.reshape(bsz, seq, d).astype(x.dtype)
```

```python
import functools
import math

import jax
import jax.numpy as jnp
from jax import lax
from jax.experimental import pallas as pl
from jax.experimental.pallas import tpu as pltpu

_F32 = jnp.float32
_BF16 = jnp.bfloat16
_EPS = 1e-6
_NEG_INF = -1e30

_LANES = 128
_SUBLANES = 8
_VMEM_LIMIT_BYTES = 60 * 1024 * 1024

_S5_GROUP_CH = 16
_S5_STATE = 64
_HEAD_DIM = 128
_N_KV_HEADS = 4
_DILATED_PATTERNS = ((128, 1), (512, 4), (2048, 16))
_GROUPS_PER_TILE = _LANES // _S5_GROUP_CH
_HALF = _GROUPS_PER_TILE * _S5_STATE
_STATE_W = 2 * _HALF


def _params(sem):
    return pltpu.CompilerParams(dimension_semantics=sem, vmem_limit_bytes=_VMEM_LIMIT_BYTES)


def _rms_scale(x):
    return lax.rsqrt(jnp.mean(x * x, axis=-1, keepdims=True) + _EPS)


def _sigmoid(v):
    return 1.0 / (1.0 + jnp.exp(-v))


def _cmul(ar, ai, br, bi):
    return ar * br - ai * bi, ar * bi + ai * br


def _s5_kernel(x_ref, g_ref, dsk_ref, lam_ref, wb_ref, wc_ref, p_ref, pt_ref, wglu_ref, o_ref,
               h_ref, hbp_ref, y_ref, bu0_ref, bu1_ref, xb0_ref, xb1_ref, carry_ref, cp_ref, lvl_ref,
               zprev_ref, xprev_ref, *, lc, tiles_per_seq):
    s = pl.program_id(0)
    n_tiles = lam_ref.shape[0]
    d_model = o_ref.shape[1]
    n_glu = n_tiles // 2
    glu_w = d_model // n_glu
    rows = lax.broadcasted_iota(jnp.int32, (_SUBLANES, _HALF), 0)

    @pl.when(s % tiles_per_seq == 0)
    def _new_sequence():
        carry_ref[...] = jnp.zeros_like(carry_ref)

    @pl.when(s == 0)
    def _init():
        zprev_ref[...] = jnp.zeros_like(zprev_ref)
        xprev_ref[...] = jnp.zeros_like(xprev_ref)

        def init_m(m, c):
            pr = lam_ref[m, :, 0:_HALF]
            pi = lam_ref[m, :, _HALF:_STATE_W]
            for _ in range(int(math.log2(lc))):
                pr, pi = _cmul(pr, pi, pr, pi)
            qr, qi = pr, pi
            for lv in range(3):
                lvl_ref[m, lv * _SUBLANES:(lv + 1) * _SUBLANES, 0:_HALF] = qr
                lvl_ref[m, lv * _SUBLANES:(lv + 1) * _SUBLANES, _HALF:_STATE_W] = qi
                qr, qi = _cmul(qr, qi, qr, qi)
            cr, ci = pr, pi
            outr, outi = pr, pi
            for c_idx in range(1, _SUBLANES):
                cr, ci = _cmul(cr, ci, pr, pi)
                outr = jnp.where(rows == c_idx, cr, outr)
                outi = jnp.where(rows == c_idx, ci, outi)
            cp_ref[m, :, 0:_HALF] = outr
            cp_ref[m, :, _HALF:_STATE_W] = outi
            return c

        lax.fori_loop(0, n_tiles, init_m, 0)

    x = x_ref[...]
    h = x * _rms_scale(x) * g_ref[...]
    h_ref[...] = h
    hp = jnp.dot(p_ref[...], h.astype(_BF16), preferred_element_type=_F32)
    for m in range(n_tiles):
        hbp_ref[m] = hp[:, m * _LANES:(m + 1) * _LANES].astype(_BF16)

    def project_in(m, bu_ref):
        bu_ref[...] = jnp.dot(hbp_ref[m], wb_ref[m], preferred_element_type=_F32)

    def project_out(m, xb_ref):
        y_ref[m] = jnp.dot(xb_ref[...], wc_ref[m], preferred_element_type=_F32)

    def scan(m, bu_ref, xb_ref):
        lr = lam_ref[m, :, 0:_HALF]
        li = lam_ref[m, :, _HALF:_STATE_W]

        xr = jnp.zeros((_SUBLANES, _HALF), _F32)
        xi = jnp.zeros((_SUBLANES, _HALF), _F32)
        for tau in range(lc):
            r0 = tau * _SUBLANES
            tr, ti = _cmul(lr, li, xr, xi)
            xr = tr + bu_ref[r0:r0 + _SUBLANES, 0:_HALF]
            xi = ti + bu_ref[r0:r0 + _SUBLANES, _HALF:_STATE_W]
            bu_ref[r0:r0 + _SUBLANES, 0:_HALF] = xr
            bu_ref[r0:r0 + _SUBLANES, _HALF:_STATE_W] = xi

        zr, zi = xr, xi
        for lv, s in enumerate((1, 2, 4)):
            ar = lvl_ref[m, lv * _SUBLANES:(lv + 1) * _SUBLANES, 0:_HALF]
            ai = lvl_ref[m, lv * _SUBLANES:(lv + 1) * _SUBLANES, _HALF:_STATE_W]
            sr = jnp.where(rows >= s, pltpu.roll(zr, s, 0), 0.0)
            si = jnp.where(rows >= s, pltpu.roll(zi, s, 0), 0.0)
            tr, ti = _cmul(ar, ai, sr, si)
            zr = zr + tr
            zi = zi + ti
        pr = carry_ref[m, :, 0:_HALF]
        pi = carry_ref[m, :, _HALF:_STATE_W]
        tr, ti = _cmul(cp_ref[m, :, 0:_HALF], cp_ref[m, :, _HALF:_STATE_W], pr, pi)
        er = zr + tr
        ei = zi + ti
        cr = jnp.where(rows >= 1, pltpu.roll(er, 1, 0), pr)
        ci = jnp.where(rows >= 1, pltpu.roll(ei, 1, 0), pi)
        carry_ref[m, :, 0:_HALF] = jnp.broadcast_to(er[_SUBLANES - 1:_SUBLANES, :], (_SUBLANES, _HALF))
        carry_ref[m, :, _HALF:_STATE_W] = jnp.broadcast_to(ei[_SUBLANES - 1:_SUBLANES, :], (_SUBLANES, _HALF))

        for tau in range(0, lc, 2):
            r0 = tau * _SUBLANES
            cr, ci = _cmul(lr, li, cr, ci)
            x0r = bu_ref[r0:r0 + _SUBLANES, 0:_HALF] + cr
            x0i = bu_ref[r0:r0 + _SUBLANES, _HALF:_STATE_W] + ci
            cr, ci = _cmul(lr, li, cr, ci)
            x1r = bu_ref[r0 + _SUBLANES:r0 + 2 * _SUBLANES, 0:_HALF] + cr
            x1i = bu_ref[r0 + _SUBLANES:r0 + 2 * _SUBLANES, _HALF:_STATE_W] + ci
            xb_ref[r0:r0 + 2 * _SUBLANES, 0:_HALF] = jnp.concatenate([x0r, x1r], axis=0).astype(_BF16)
            xb_ref[r0:r0 + 2 * _SUBLANES, _HALF:_STATE_W] = jnp.concatenate([x0i, x1i], axis=0).astype(_BF16)

    def glu_value(k):
        cs = slice(k * glu_w, (k + 1) * glu_w)
        o_ref[:, cs] = jnp.dot(zprev_ref[...], wglu_ref[:, cs], preferred_element_type=_F32)

    def glu_gate(k):
        cs = slice(k * glu_w, (k + 1) * glu_w)
        gate = jnp.dot(zprev_ref[...], wglu_ref[:, d_model + k * glu_w:d_model + (k + 1) * glu_w],
                       preferred_element_type=_F32)
        o_ref[:, cs] = xprev_ref[:, cs] + o_ref[:, cs] * _sigmoid(gate)

    project_in(0, bu0_ref)
    for k in range(n_glu):
        m0 = 2 * k
        project_in(m0 + 1, bu1_ref)
        if k > 0:
            project_out(m0 - 1, xb1_ref)
        glu_value(k)
        scan(m0, bu0_ref, xb0_ref)
        if m0 + 2 < n_tiles:
            project_in(m0 + 2, bu0_ref)
        project_out(m0, xb0_ref)
        glu_gate(k)
        scan(m0 + 1, bu1_ref, xb1_ref)
    project_out(n_tiles - 1, xb1_ref)

    y = jnp.concatenate([y_ref[m] for m in range(n_tiles)], axis=1)
    y_hi = y.astype(_BF16)
    y_lo = (y - y_hi.astype(_F32)).astype(_BF16)
    pt = pt_ref[...]
    yn = (jnp.dot(pt, y_hi, preferred_element_type=_F32)
          + jnp.dot(pt, y_lo, preferred_element_type=_F32))
    yn = yn + dsk_ref[...] * h_ref[...]
    cdf = 0.5 * (1.0 + jnp.tanh(math.sqrt(2.0 / math.pi) * (yn + 0.044715 * (yn * yn * yn))))
    zprev_ref[...] = (yn * cdf).astype(zprev_ref.dtype)
    xprev_ref[...] = x_ref[...]


def _s5_tables(lam_re, lam_im, log_dt, b_re, b_im, c_re, c_im):
    g, p = lam_re.shape
    c = b_re.shape[-1]
    nm = g // _GROUPS_PER_TILE
    lr = lam_re.astype(_F32)
    li = lam_im.astype(_F32)
    dt = jnp.exp(log_dt.astype(_F32))[:, None]
    mag = jnp.exp(lr * dt)
    ang = li * dt
    lb_re = mag * jnp.cos(ang)
    lb_im = mag * jnp.sin(ang)
    nr = lb_re - 1.0
    den = lr * lr + li * li
    f_re = (nr * lr + lb_im * li) / den
    f_im = (lb_im * lr - nr * li) / den
    br = b_re.astype(_F32)
    bi = b_im.astype(_F32)
    bb_re = f_re[..., None] * br - f_im[..., None] * bi
    bb_im = f_re[..., None] * bi + f_im[..., None] * br
    eye = jnp.eye(_GROUPS_PER_TILE, dtype=_F32)

    def blk_b(bb):
        t = bb.reshape(nm, _GROUPS_PER_TILE, p, c)
        return jnp.einsum('mgpc,gh->mgchp', t, eye).reshape(nm, _GROUPS_PER_TILE * c, _GROUPS_PER_TILE * p)

    def blk_c(cc):
        t = cc.reshape(nm, _GROUPS_PER_TILE, c, p)
        return jnp.einsum('mgcp,gh->mgphc', t, eye).reshape(nm, _GROUPS_PER_TILE * p, _GROUPS_PER_TILE * c)

    wb = jnp.concatenate([blk_b(bb_re), blk_b(bb_im)], axis=-1).astype(_BF16)
    wc = jnp.concatenate([blk_c(c_re.astype(_F32)), blk_c(-c_im.astype(_F32))], axis=1).astype(_BF16)
    lam = jnp.concatenate([lb_re.reshape(nm, _HALF), lb_im.reshape(nm, _HALF)], axis=-1)
    lam8 = jnp.broadcast_to(lam[:, None, :], (nm, _SUBLANES, _STATE_W))
    return lam8, wb, wc


def _s5_glu_layer(x2, bsz, seq, gain, d_skip, lam8, wb, wc, w_glu, *, tile=256):
    t_tok, d = x2.shape
    nm = lam8.shape[0]
    lc = tile // _SUBLANES
    nt = seq // tile
    n_steps = bsz * nt
    assert (d // (nm // 2)) % _LANES == 0
    r = jnp.arange(tile)
    col = (r % _SUBLANES) * lc + r // _SUBLANES
    perm = (col[:, None] == jnp.arange(tile)[None, :]).astype(_BF16)
    perm_t = perm.T
    const3 = lambda s: (0, 0, 0)
    const2 = lambda s: (0, 0)
    once = dict(pipeline_mode=pl.Buffered(1))
    return pl.pallas_call(
        functools.partial(_s5_kernel, lc=lc, tiles_per_seq=nt),
        grid=(n_steps + 1,),
        in_specs=[
            pl.BlockSpec((tile, d), lambda s: (jnp.minimum(s, n_steps - 1), 0)),
            pl.BlockSpec((1, d), const2),
            pl.BlockSpec((1, d), const2),
            pl.BlockSpec((nm, _SUBLANES, _STATE_W), const3, **once),
            pl.BlockSpec((nm, _LANES, _STATE_W), const3, **once),
            pl.BlockSpec((nm, _STATE_W, _LANES), const3, **once),
            pl.BlockSpec((tile, tile), const2),
            pl.BlockSpec((tile, tile), const2),
            pl.BlockSpec((d, 2 * d), const2, **once),
        ],
        out_specs=pl.BlockSpec((tile, d), lambda s: (jnp.maximum(s - 1, 0), 0)),
        out_shape=jax.ShapeDtypeStruct((t_tok, d), _F32),
        scratch_shapes=[
            pltpu.VMEM((tile, d), _F32),
            pltpu.VMEM((nm, tile, _LANES), _BF16),
            pltpu.VMEM((nm, tile, _LANES), _F32),
            pltpu.VMEM((tile, _STATE_W), _F32),
            pltpu.VMEM((tile, _STATE_W), _F32),
            pltpu.VMEM((tile, _STATE_W), _BF16),
            pltpu.VMEM((tile, _STATE_W), _BF16),
            pltpu.VMEM((nm, _SUBLANES, _STATE_W), _F32),
            pltpu.VMEM((nm, _SUBLANES, _STATE_W), _F32),
            pltpu.VMEM((nm, 3 * _SUBLANES, _STATE_W), _F32),
            pltpu.VMEM((tile, d), _BF16),
            pltpu.VMEM((tile, d), _F32),
        ],
        compiler_params=_params(("arbitrary",)),
        name="s5_glu_layer",
    )(x2, gain.reshape(1, d).astype(_F32), d_skip.reshape(1, d).astype(_F32), lam8, wb, wc, perm, perm_t, w_glu)


def _ffn_kernel(x_ref, g_ref, wg_ref, wu_ref, wo_ref, fg_ref, o_ref, hn_ref, *, final_norm, row_split):
    f = pl.program_id(1)
    rows = o_ref.shape[0] // row_split
    slabs = [slice(s * rows, (s + 1) * rows) for s in range(row_split)]

    @pl.when(f == 0)
    def _start():
        for sl in slabs:
            x = x_ref[sl, :]
            hn_ref[sl, :] = (x * _rms_scale(x) * g_ref[...]).astype(_BF16)
            o_ref[sl, :] = x

    for sl in slabs:
        hn = hn_ref[sl, :]
        a = jnp.dot(hn, wg_ref[...], preferred_element_type=_F32)
        u = jnp.dot(hn, wu_ref[...], preferred_element_type=_F32)
        act = (a * _sigmoid(a) * u).astype(_BF16)
        o_ref[sl, :] += jnp.dot(act, wo_ref[...], preferred_element_type=_F32)

    if final_norm:
        @pl.when(f == pl.num_programs(1) - 1)
        def _finish():
            for sl in slabs:
                y = o_ref[sl, :]
                o_ref[sl, :] = y * _rms_scale(y) * fg_ref[...]


def _ffn_residual(x2, gain, w_in, w_out, layer, final_gain=None, *, tm=1024, tf=512, row_split=2):
    t_tok, d = x2.shape
    hidden = w_out.shape[1]
    nf = hidden // tf
    final_norm = final_gain is not None
    fg = (final_gain if final_norm else gain).reshape(1, d).astype(_F32)
    return pl.pallas_call(
        functools.partial(_ffn_kernel, final_norm=final_norm, row_split=row_split),
        grid=(t_tok // tm, nf),
        in_specs=[
            pl.BlockSpec((tm, d), lambda i, f: (i, 0)),
            pl.BlockSpec((1, d), lambda i, f: (0, 0)),
            pl.BlockSpec((None, d, tf), lambda i, f: (layer, 0, f)),
            pl.BlockSpec((None, d, tf), lambda i, f: (layer, 0, f + nf)),
            pl.BlockSpec((None, tf, d), lambda i, f: (layer, f, 0)),
            pl.BlockSpec((1, d), lambda i, f: (0, 0)),
        ],
        out_specs=pl.BlockSpec((tm, d), lambda i, f: (i, 0)),
        out_shape=jax.ShapeDtypeStruct((t_tok, d), _F32),
        scratch_shapes=[pltpu.VMEM((tm, d), _BF16)],
        compiler_params=_params(("parallel", "arbitrary")),
        name="ffn_residual",
    )(x2, gain.reshape(1, d).astype(_F32), w_in, w_in, w_out, fg)


_PERM_ROWS = 256


def _perm_matrix(dil, rows=_PERM_ROWS):
    dst = jnp.arange(rows)
    src = (dst % (rows // dil)) * dil + dst // (rows // dil)
    return (src[:, None] == jnp.arange(rows)[None, :]).astype(_BF16)


def _to_residue_major(y, p_ref, dil, store):
    tm = y.shape[0]
    run = _PERM_ROWS // dil
    for j in range(tm // _PERM_ROWS):
        yp = jnp.dot(p_ref[...], y[j * _PERM_ROWS:(j + 1) * _PERM_ROWS], preferred_element_type=_F32).astype(_BF16)
        for r in range(dil):
            store(r, j * run, yp[r * run:(r + 1) * run])


def _qkv_kernel(*refs, dils, nj):
    n_groups = len(dils)
    n_p = sum(1 for dl in dils if dl > 1)
    x_ref, gq_ref, gkv_ref, w_ref = refs[:4]
    p_refs = refs[4:4 + n_p]
    q_refs = refs[4 + n_p:4 + n_p + n_groups]
    kv_refs = refs[4 + n_p + n_groups:4 + n_p + 2 * n_groups]
    hn_ref = refs[-1]
    tm = hn_ref.shape[1]
    p_of = {}
    for dl in dils:
        if dl > 1:
            p_of[dl] = p_refs[len(p_of)]
    j = pl.program_id(2)

    @pl.when(j == 0)
    def _start():
        for sb in range(tm // _PERM_ROWS):
            rows = slice(sb * _PERM_ROWS, (sb + 1) * _PERM_ROWS)
            x = x_ref[rows, :]
            xr = x * _rms_scale(x)
            hq = (xr * gq_ref[...]).astype(_BF16)
            hn_ref[n_groups, rows, :] = (xr * gkv_ref[...]).astype(_BF16)
            for g, dl in enumerate(dils):
                if dl == 1:
                    hn_ref[g, rows, :] = hq
                else:
                    run = _PERM_ROWS // dl
                    yp = jnp.dot(p_of[dl][...], hq, preferred_element_type=_F32).astype(_BF16)
                    for r in range(dl):
                        dst = r * (tm // dl) + sb * run
                        hn_ref[g, dst:dst + run, :] = yp[r * run:(r + 1) * run]

    for g, dl in enumerate(dils):
        @pl.when((j >= g * nj) & (j < (g + 1) * nj))
        def _q_tile(g=g, dl=dl):
            res = jnp.dot(hn_ref[g], w_ref[...], preferred_element_type=_F32).astype(_BF16)
            for r in range(dl):
                q_refs[g][r] = res[r * (tm // dl):(r + 1) * (tm // dl)]

    @pl.when(j >= n_groups * nj)
    def _kv_tile():
        kv = jnp.dot(hn_ref[n_groups], w_ref[...], preferred_element_type=_F32).astype(_BF16)
        for g, dl in enumerate(dils):
            if dl == 1:
                kv_refs[g][0] = kv
            else:
                def store(r, row, block, g=g):
                    kv_refs[g][r, row:row + block.shape[0], :] = block
                _to_residue_major(kv, p_of[dl], dl, store)


def _qkv_project(x2, bsz, seq, q_gain, kv_gain, w_cat, qw, dils, *, tm=1024, tn=512):
    t_tok, d = x2.shape
    n_groups = len(dils)
    kvw = w_cat.shape[1] - n_groups * qw
    assert qw % tn == 0 and kvw % tn == 0
    nj = qw // tn
    nkv = kvw // tn
    nt = seq // tm
    perms = [_perm_matrix(dl) for dl in dils if dl > 1]

    def q_spec(g, dl):
        return pl.BlockSpec((None, dl, tm // dl, tn),
                            lambda b, t, j: (b, 0, t, jnp.clip(j - g * nj, 0, nj - 1)))

    outs = pl.pallas_call(
        functools.partial(_qkv_kernel, dils=tuple(dils), nj=nj),
        grid=(bsz, nt, n_groups * nj + nkv),
        in_specs=([pl.BlockSpec((tm, d), lambda b, t, j: (b * nt + t, 0)),
                   pl.BlockSpec((1, d), lambda b, t, j: (0, 0)),
                   pl.BlockSpec((1, d), lambda b, t, j: (0, 0)),
                   pl.BlockSpec((d, tn), lambda b, t, j: (0, j))]
                  + [pl.BlockSpec((_PERM_ROWS, _PERM_ROWS), lambda b, t, j: (0, 0))] * len(perms)),
        out_specs=([q_spec(g, dl) for g, dl in enumerate(dils)]
                   + [pl.BlockSpec((None, dl, tm // dl, tn),
                                   lambda b, t, j: (b, 0, t, jnp.clip(j - n_groups * nj, 0, nkv - 1)))
                      for dl in dils]),
        out_shape=([jax.ShapeDtypeStruct((bsz, dl, seq // dl, qw), _BF16) for dl in dils]
                   + [jax.ShapeDtypeStruct((bsz, dl, seq // dl, kvw), _BF16) for dl in dils]),
        scratch_shapes=[pltpu.VMEM((n_groups + 1, tm, d), _BF16)],
        compiler_params=_params(("parallel", "parallel", "arbitrary")),
        name="qkv_project",
    )(x2, q_gain.reshape(1, d).astype(_F32), kv_gain.reshape(1, d).astype(_F32), w_cat, *perms)
    return outs[:n_groups], outs[n_groups:]


def _attn_kernel(q_ref, kp_ref, kc_ref, vp_ref, vc_ref, o_ref, st_ref, *, blk, rep, scale):
    i = pl.program_id(2)
    nq = q_ref.shape[0] // blk
    qi = lax.broadcasted_iota(jnp.int32, (blk, 2 * blk), 0)
    si = lax.broadcasted_iota(jnp.int32, (blk, 2 * blk), 1)
    dist = qi + blk - si
    band = (dist >= 0) & (dist <= blk)
    band_first = band & ((i > 0) | (si >= blk))
    lane = lax.broadcasted_iota(jnp.int32, (blk, _LANES), 1)
    n_kv = kc_ref.shape[1] // _HEAD_DIM
    for sub in range(nq):
        valid = jnp.concatenate([band_first if sub == 0 else band] * rep, axis=0)
        rows = slice(sub * blk, (sub + 1) * blk)
        st = jnp.zeros((blk, _LANES), _F32)
        for kvh in range(n_kv):
            ks = slice(kvh * _HEAD_DIM, (kvh + 1) * _HEAD_DIM)
            if sub == 0:
                k = jnp.concatenate([kp_ref[:, ks], kc_ref[0:blk, ks]], axis=0)
                v = jnp.concatenate([vp_ref[:, ks], vc_ref[0:blk, ks]], axis=0)
            else:
                k = kc_ref[(sub - 1) * blk:(sub + 1) * blk, ks]
                v = vc_ref[(sub - 1) * blk:(sub + 1) * blk, ks]
            q = jnp.concatenate(
                [q_ref[rows, (kvh * rep + rp) * _HEAD_DIM:(kvh * rep + rp + 1) * _HEAD_DIM] for rp in range(rep)],
                axis=0)
            s = lax.dot_general(q, k, (((1,), (1,)), ((), ())), preferred_element_type=_F32) * scale
            s = jnp.where(valid, s, _NEG_INF)
            m = jnp.max(s, axis=-1, keepdims=True)
            p = jnp.exp(s - m)
            l = jnp.sum(p, axis=-1, keepdims=True)
            pn = (p * (1.0 / l)).astype(_BF16)
            o = jnp.dot(pn, v, preferred_element_type=_F32)
            lse = m + jnp.log(l)
            for rp in range(rep):
                hd = kvh * rep + rp
                o_ref[rows, hd * _HEAD_DIM:(hd + 1) * _HEAD_DIM] = o[rp * blk:(rp + 1) * blk].astype(o_ref.dtype)
                st = jnp.where(lane == hd, lse[rp * blk:(rp + 1) * blk], st)
        st_ref[rows, :] = st


def _dilated_attention(q, kv, window, dilation, *, nq=2):
    bsz, dil, n, qw = q.shape
    assert dil == dilation
    kw = kv.shape[3] // 2
    n_heads = qw // _HEAD_DIM
    rep = n_heads // (kw // _HEAD_DIM)
    blk = window // dilation
    assert n % (nq * blk) == 0 and blk == _LANES and n_heads <= _LANES
    nb = n // (nq * blk)
    prev = lambda i: jnp.maximum(nq * i - 1, 0)
    return pl.pallas_call(
        functools.partial(_attn_kernel, blk=blk, rep=rep, scale=_HEAD_DIM ** -0.5),
        grid=(bsz, dilation, nb),
        in_specs=[
            pl.BlockSpec((None, None, nq * blk, qw), lambda b, r, i: (b, r, i, 0)),
            pl.BlockSpec((None, None, blk, kw), lambda b, r, i: (b, r, prev(i), 0)),
            pl.BlockSpec((None, None, nq * blk, kw), lambda b, r, i: (b, r, i, 0)),
            pl.BlockSpec((None, None, blk, kw), lambda b, r, i: (b, r, prev(i), 1)),
            pl.BlockSpec((None, None, nq * blk, kw), lambda b, r, i: (b, r, i, 1)),
        ],
        out_specs=[
            pl.BlockSpec((None, None, nq * blk, qw), lambda b, r, i: (b, r, i, 0)),
            pl.BlockSpec((None, None, nq * blk, _LANES), lambda b, r, i: (b, r, i, 0)),
        ],
        out_shape=[
            jax.ShapeDtypeStruct((bsz, dilation, n, qw), _BF16),
            jax.ShapeDtypeStruct((bsz, dilation, n, _LANES), _F32),
        ],
        compiler_params=_params(("parallel", "parallel", "arbitrary")),
        name=f"dilated_attention_d{dilation}",
    )(q, kv, kv, kv, kv)


def _combine_kernel(*refs, dils, n_heads):
    n_groups = len(dils)
    n_p = sum(1 for dl in dils if dl > 1)
    o_in = refs[:n_groups]
    s_in = refs[n_groups:2 * n_groups]
    x_ref, wo_ref = refs[2 * n_groups:2 * n_groups + 2]
    pt_refs = refs[2 * n_groups + 2:2 * n_groups + 2 + n_p]
    out_ref, onat_ref, snat_ref, comb_ref = refs[2 * n_groups + 2 + n_p:]
    tm = x_ref.shape[0]

    p_of = {}
    for dl in dils:
        if dl > 1:
            p_of[dl] = pt_refs[len(p_of)]

    for j in range(tm // _PERM_ROWS):
        rs = slice(j * _PERM_ROWS, (j + 1) * _PERM_ROWS)
        for g, dl in enumerate(dils):
            if dl == 1:
                onat_ref[g, rs, :] = o_in[g][0, rs, :]
                snat_ref[g, rs, :] = s_in[g][0, rs, :]
                continue
            run = _PERM_ROWS // dl
            pt = p_of[dl][...]
            oc = jnp.concatenate([o_in[g][r, j * run:(j + 1) * run, :] for r in range(dl)], axis=0)
            onat_ref[g, rs, :] = jnp.dot(pt, oc, preferred_element_type=_F32).astype(_BF16)
            sc = jnp.concatenate([s_in[g][r, j * run:(j + 1) * run, :] for r in range(dl)], axis=0)
            s_hi = sc.astype(_BF16)
            rem = sc - s_hi.astype(_F32)
            s_mid = rem.astype(_BF16)
            s_lo = (rem - s_mid.astype(_F32)).astype(_BF16)
            snat_ref[g, rs, :] = (jnp.dot(pt, s_hi, preferred_element_type=_F32)
                                  + jnp.dot(pt, s_mid, preferred_element_type=_F32)
                                  + jnp.dot(pt, s_lo, preferred_element_type=_F32))

        lses = [snat_ref[g, rs, :] for g in range(n_groups)]
        mx = lses[0]
        for l in lses[1:]:
            mx = jnp.maximum(mx, l)
        es = [jnp.exp(l - mx) for l in lses]
        den = es[0]
        for e in es[1:]:
            den = den + e
        inv = 1.0 / den
        wts = [e * inv for e in es]
        for hd in range(n_heads):
            cs = slice(hd * _HEAD_DIM, (hd + 1) * _HEAD_DIM)
            base = onat_ref[0, rs, cs].astype(_F32)
            acc = base
            for g in range(1, n_groups):
                acc = acc + wts[g][:, hd:hd + 1] * (onat_ref[g, rs, cs].astype(_F32) - base)
            comb_ref[rs, cs] = acc.astype(_BF16)
        out_ref[rs, :] = x_ref[rs, :] + jnp.dot(comb_ref[rs, :], wo_ref[...], preferred_element_type=_F32)


def _combine_project(outs, stats, x2, w_o, layer, dils, *, tm=512):
    t_tok, d = x2.shape
    bsz, _, _, qw = outs[0].shape
    seq = t_tok // bsz
    nt = seq // tm
    n_groups = len(outs)
    perms_t = [_perm_matrix(dl).T for dl in dils if dl > 1]
    grp = lambda b, t: (b, 0, t, 0)
    return pl.pallas_call(
        functools.partial(_combine_kernel, dils=tuple(dils), n_heads=qw // _HEAD_DIM),
        grid=(bsz, nt),
        in_specs=([pl.BlockSpec((None, dl, tm // dl, qw), grp) for dl in dils]
                  + [pl.BlockSpec((None, dl, tm // dl, _LANES), grp) for dl in dils]
                  + [pl.BlockSpec((tm, d), lambda b, t: (b * nt + t, 0)),
                     pl.BlockSpec((None, qw, d), lambda b, t: (layer, 0, 0))]
                  + [pl.BlockSpec((_PERM_ROWS, _PERM_ROWS), lambda b, t: (0, 0))] * len(perms_t)),
        out_specs=pl.BlockSpec((tm, d), lambda b, t: (b * nt + t, 0)),
        out_shape=jax.ShapeDtypeStruct((t_tok, d), _F32),
        scratch_shapes=[pltpu.VMEM((n_groups, tm, qw), _BF16),
                        pltpu.VMEM((n_groups, tm, _LANES), _F32),
                        pltpu.VMEM((tm, qw), _BF16)],
        compiler_params=_params(("parallel", "arbitrary")),
        name="combine_project",
    )(*outs, *stats, x2, w_o, *perms_t)


def kernel(x, s5_lam_re, s5_lam_im, s5_log_dt, s5_b_re, s5_b_im, s5_c_re, s5_c_im, s5_d, s5_w_glu, a_norm_mix, ffn_norm, ffn_w_in, ffn_w_out, b_norm_mix, attn_w_q, attn_w_o, kv_norm, w_kv, final_norm):
    bsz, seq, d = x.shape
    n_a = a_norm_mix.shape[0]
    n_b = b_norm_mix.shape[0]
    depth = n_a + n_b
    n_groups = len(_DILATED_PATTERNS)
    dils = [dl for _, dl in _DILATED_PATTERNS]
    x2 = x.reshape(bsz * seq, d).astype(_F32)
    w_in_b = ffn_w_in.astype(_BF16)
    w_out_b = ffn_w_out.astype(_BF16)
    w_o_b = attn_w_o.astype(_BF16)
    assert n_b == 1, "the fused q/kv projection covers the single attention layer of this model"
    qw = attn_w_q.shape[2] // n_groups
    for layer in range(depth):
        if layer < n_a:
            i = layer
            lam8, wb, wc = _s5_tables(s5_lam_re[i], s5_lam_im[i], s5_log_dt[i], s5_b_re[i], s5_b_im[i],
                                      s5_c_re[i], s5_c_im[i])
            x2 = _s5_glu_layer(x2, bsz, seq, a_norm_mix[i], s5_d[i], lam8, wb, wc, s5_w_glu[i].astype(_BF16))
        else:
            j = layer - n_a
            w_cat = jnp.concatenate([attn_w_q[j], w_kv], axis=1).astype(_BF16)
            qs, kvs = _qkv_project(x2, bsz, seq, b_norm_mix[j], kv_norm, w_cat, qw, dils)
            outs, stats = [], []
            for g, (window, dilation) in enumerate(_DILATED_PATTERNS):
                o, st = _dilated_attention(qs[g], kvs[g], window, dilation)
                outs.append(o)
                stats.append(st)
            x2 = _combine_project(outs, stats, x2, w_o_b, j, dils)
        final_gain = final_norm if layer == depth - 1 else None
        x2 = _ffn_residual(x2, ffn_norm[layer], w_in_b, w_out_b, layer, final_gain)
    return x2.reshape(bsz, seq, d).astype(x.dtype)
```

```python
import functools
import math

import jax
import jax.numpy as jnp
from jax import lax
from jax.experimental import pallas as pl
from jax.experimental.pallas import tpu as pltpu

_F32 = jnp.float32
_BF16 = jnp.bfloat16
_EPS = 1e-6
_NEG_INF = -1e30

_LANES = 128
_SUBLANES = 8
_VMEM_LIMIT_BYTES = 60 * 1024 * 1024

_S5_GROUP_CH = 16
_S5_STATE = 64
_HEAD_DIM = 128
_N_KV_HEADS = 4
_DILATED_PATTERNS = ((128, 1), (512, 4), (2048, 16))
_GROUPS_PER_TILE = _LANES // _S5_GROUP_CH
_HALF = _GROUPS_PER_TILE * _S5_STATE
_STATE_W = 2 * _HALF


def _params(sem):
    return pltpu.CompilerParams(dimension_semantics=sem, vmem_limit_bytes=_VMEM_LIMIT_BYTES)


def _rms_scale(x):
    return lax.rsqrt(jnp.mean(x * x, axis=-1, keepdims=True) + _EPS)


def _sigmoid(v):
    return 1.0 / (1.0 + jnp.exp(-v))


def _cmul(ar, ai, br, bi):
    return ar * br - ai * bi, ar * bi + ai * br


def _s5_kernel(x_ref, g_ref, dsk_ref, lam_ref, wb_ref, wc_ref, p_ref, pt_ref, wglu_ref, o_ref,
               h_ref, hbp_ref, y_ref, bu0_ref, bu1_ref, xb0_ref, xb1_ref, carry_ref, cp_ref, lvl_ref,
               zprev_ref, xprev_ref, *, lc, tiles_per_seq):
    s = pl.program_id(0)
    n_tiles = lam_ref.shape[0]
    d_model = o_ref.shape[1]
    n_glu = n_tiles // 2
    glu_w = d_model // n_glu
    rows = lax.broadcasted_iota(jnp.int32, (_SUBLANES, _HALF), 0)

    @pl.when(s % tiles_per_seq == 0)
    def _new_sequence():
        carry_ref[...] = jnp.zeros_like(carry_ref)

    @pl.when(s == 0)
    def _init():
        zprev_ref[...] = jnp.zeros_like(zprev_ref)
        xprev_ref[...] = jnp.zeros_like(xprev_ref)

        def init_m(m, c):
            pr = lam_ref[m, :, 0:_HALF]
            pi = lam_ref[m, :, _HALF:_STATE_W]
            for _ in range(int(math.log2(lc))):
                pr, pi = _cmul(pr, pi, pr, pi)
            qr, qi = pr, pi
            for lv in range(3):
                lvl_ref[m, lv * _SUBLANES:(lv + 1) * _SUBLANES, 0:_HALF] = qr
                lvl_ref[m, lv * _SUBLANES:(lv + 1) * _SUBLANES, _HALF:_STATE_W] = qi
                qr, qi = _cmul(qr, qi, qr, qi)
            cr, ci = pr, pi
            outr, outi = pr, pi
            for c_idx in range(1, _SUBLANES):
                cr, ci = _cmul(cr, ci, pr, pi)
                outr = jnp.where(rows == c_idx, cr, outr)
                outi = jnp.where(rows == c_idx, ci, outi)
            cp_ref[m, :, 0:_HALF] = outr
            cp_ref[m, :, _HALF:_STATE_W] = outi
            return c

        lax.fori_loop(0, n_tiles, init_m, 0)

    x = x_ref[...]
    h = x * _rms_scale(x) * g_ref[...]
    h_ref[...] = h
    hp = jnp.dot(p_ref[...], h.astype(_BF16), preferred_element_type=_F32)
    for m in range(n_tiles):
        hbp_ref[m] = hp[:, m * _LANES:(m + 1) * _LANES].astype(_BF16)

    def project_in(m, bu_ref):
        bu_ref[...] = jnp.dot(hbp_ref[m], wb_ref[m], preferred_element_type=_F32)

    def project_out(m, xb_ref):
        y_ref[m] = jnp.dot(xb_ref[...], wc_ref[m], preferred_element_type=_F32)

    def scan(m, bu_ref, xb_ref):
        lr = lam_ref[m, :, 0:_HALF]
        li = lam_ref[m, :, _HALF:_STATE_W]

        xr = jnp.zeros((_SUBLANES, _HALF), _F32)
        xi = jnp.zeros((_SUBLANES, _HALF), _F32)
        for tau in range(lc):
            r0 = tau * _SUBLANES
            tr, ti = _cmul(lr, li, xr, xi)
            xr = tr + bu_ref[r0:r0 + _SUBLANES, 0:_HALF]
            xi = ti + bu_ref[r0:r0 + _SUBLANES, _HALF:_STATE_W]
            bu_ref[r0:r0 + _SUBLANES, 0:_HALF] = xr
            bu_ref[r0:r0 + _SUBLANES, _HALF:_STATE_W] = xi

        zr, zi = xr, xi
        for lv, s in enumerate((1, 2, 4)):
            ar = lvl_ref[m, lv * _SUBLANES:(lv + 1) * _SUBLANES, 0:_HALF]
            ai = lvl_ref[m, lv * _SUBLANES:(lv + 1) * _SUBLANES, _HALF:_STATE_W]
            sr = jnp.where(rows >= s, pltpu.roll(zr, s, 0), 0.0)
            si = jnp.where(rows >= s, pltpu.roll(zi, s, 0), 0.0)
            tr, ti = _cmul(ar, ai, sr, si)
            zr = zr + tr
            zi = zi + ti
        pr = carry_ref[m, :, 0:_HALF]
        pi = carry_ref[m, :, _HALF:_STATE_W]
        tr, ti = _cmul(cp_ref[m, :, 0:_HALF], cp_ref[m, :, _HALF:_STATE_W], pr, pi)
        er = zr + tr
        ei = zi + ti
        cr = jnp.where(rows >= 1, pltpu.roll(er, 1, 0), pr)
        ci = jnp.where(rows >= 1, pltpu.roll(ei, 1, 0), pi)
        carry_ref[m, :, 0:_HALF] = jnp.broadcast_to(er[_SUBLANES - 1:_SUBLANES, :], (_SUBLANES, _HALF))
        carry_ref[m, :, _HALF:_STATE_W] = jnp.broadcast_to(ei[_SUBLANES - 1:_SUBLANES, :], (_SUBLANES, _HALF))

        for tau in range(0, lc, 2):
            r0 = tau * _SUBLANES
            cr, ci = _cmul(lr, li, cr, ci)
            x0r = bu_ref[r0:r0 + _SUBLANES, 0:_HALF] + cr
            x0i = bu_ref[r0:r0 + _SUBLANES, _HALF:_STATE_W] + ci
            cr, ci = _cmul(lr, li, cr, ci)
            x1r = bu_ref[r0 + _SUBLANES:r0 + 2 * _SUBLANES, 0:_HALF] + cr
            x1i = bu_ref[r0 + _SUBLANES:r0 + 2 * _SUBLANES, _HALF:_STATE_W] + ci
            xb_ref[r0:r0 + 2 * _SUBLANES, 0:_HALF] = jnp.concatenate([x0r, x1r], axis=0).astype(_BF16)
            xb_ref[r0:r0 + 2 * _SUBLANES, _HALF:_STATE_W] = jnp.concatenate([x0i, x1i], axis=0).astype(_BF16)

    def glu_value(k):
        cs = slice(k * glu_w, (k + 1) * glu_w)
        o_ref[:, cs] = jnp.dot(zprev_ref[...], wglu_ref[:, cs], preferred_element_type=_F32)

    def glu_gate(k):
        cs = slice(k * glu_w, (k + 1) * glu_w)
        gate = jnp.dot(zprev_ref[...], wglu_ref[:, d_model + k * glu_w:d_model + (k + 1) * glu_w],
                       preferred_element_type=_F32)
        o_ref[:, cs] = xprev_ref[:, cs] + o_ref[:, cs] * _sigmoid(gate)

    project_in(0, bu0_ref)
    for k in range(n_glu):
        m0 = 2 * k
        project_in(m0 + 1, bu1_ref)
        if k > 0:
            project_out(m0 - 1, xb1_ref)
        glu_value(k)
        scan(m0, bu0_ref, xb0_ref)
        if m0 + 2 < n_tiles:
            project_in(m0 + 2, bu0_ref)
        project_out(m0, xb0_ref)
        glu_gate(k)
        scan(m0 + 1, bu1_ref, xb1_ref)
    project_out(n_tiles - 1, xb1_ref)

    y = jnp.concatenate([y_ref[m] for m in range(n_tiles)], axis=1)
    y_hi = y.astype(_BF16)
    y_lo = (y - y_hi.astype(_F32)).astype(_BF16)
    pt = pt_ref[...]
    yn = (jnp.dot(pt, y_hi, preferred_element_type=_F32)
          + jnp.dot(pt, y_lo, preferred_element_type=_F32))
    yn = yn + dsk_ref[...] * h_ref[...]
    cdf = 0.5 * (1.0 + jnp.tanh(math.sqrt(2.0 / math.pi) * (yn + 0.044715 * (yn * yn * yn))))
    zprev_ref[...] = (yn * cdf).astype(zprev_ref.dtype)
    xprev_ref[...] = x_ref[...]


def _s5_tables(lam_re, lam_im, log_dt, b_re, b_im, c_re, c_im):
    g, p = lam_re.shape
    c = b_re.shape[-1]
    nm = g // _GROUPS_PER_TILE
    lr = lam_re.astype(_F32)
    li = lam_im.astype(_F32)
    dt = jnp.exp(log_dt.astype(_F32))[:, None]
    mag = jnp.exp(lr * dt)
    ang = li * dt
    lb_re = mag * jnp.cos(ang)
    lb_im = mag * jnp.sin(ang)
    nr = lb_re - 1.0
    den = lr * lr + li * li
    f_re = (nr * lr + lb_im * li) / den
    f_im = (lb_im * lr - nr * li) / den
    br = b_re.astype(_F32)
    bi = b_im.astype(_F32)
    bb_re = f_re[..., None] * br - f_im[..., None] * bi
    bb_im = f_re[..., None] * bi + f_im[..., None] * br
    eye = jnp.eye(_GROUPS_PER_TILE, dtype=_F32)

    def blk_b(bb):
        t = bb.reshape(nm, _GROUPS_PER_TILE, p, c)
        return jnp.einsum('mgpc,gh->mgchp', t, eye).reshape(nm, _GROUPS_PER_TILE * c, _GROUPS_PER_TILE * p)

    def blk_c(cc):
        t = cc.reshape(nm, _GROUPS_PER_TILE, c, p)
        return jnp.einsum('mgcp,gh->mgphc', t, eye).reshape(nm, _GROUPS_PER_TILE * p, _GROUPS_PER_TILE * c)

    wb = jnp.concatenate([blk_b(bb_re), blk_b(bb_im)], axis=-1).astype(_BF16)
    wc = jnp.concatenate([blk_c(c_re.astype(_F32)), blk_c(-c_im.astype(_F32))], axis=1).astype(_BF16)
    lam = jnp.concatenate([lb_re.reshape(nm, _HALF), lb_im.reshape(nm, _HALF)], axis=-1)
    lam8 = jnp.broadcast_to(lam[:, None, :], (nm, _SUBLANES, _STATE_W))
    return lam8, wb, wc


def _s5_glu_layer(x2, bsz, seq, gain, d_skip, lam8, wb, wc, w_glu, *, tile=256):
    t_tok, d = x2.shape
    nm = lam8.shape[0]
    lc = tile // _SUBLANES
    nt = seq // tile
    n_steps = bsz * nt
    assert (d // (nm // 2)) % _LANES == 0
    r = jnp.arange(tile)
    col = (r % _SUBLANES) * lc + r // _SUBLANES
    perm = (col[:, None] == jnp.arange(tile)[None, :]).astype(_BF16)
    perm_t = perm.T
    const3 = lambda s: (0, 0, 0)
    const2 = lambda s: (0, 0)
    once = dict(pipeline_mode=pl.Buffered(1))
    return pl.pallas_call(
        functools.partial(_s5_kernel, lc=lc, tiles_per_seq=nt),
        grid=(n_steps + 1,),
        in_specs=[
            pl.BlockSpec((tile, d), lambda s: (jnp.minimum(s, n_steps - 1), 0)),
            pl.BlockSpec((1, d), const2),
            pl.BlockSpec((1, d), const2),
            pl.BlockSpec((nm, _SUBLANES, _STATE_W), const3, **once),
            pl.BlockSpec((nm, _LANES, _STATE_W), const3, **once),
            pl.BlockSpec((nm, _STATE_W, _LANES), const3, **once),
            pl.BlockSpec((tile, tile), const2),
            pl.BlockSpec((tile, tile), const2),
            pl.BlockSpec((d, 2 * d), const2, **once),
        ],
        out_specs=pl.BlockSpec((tile, d), lambda s: (jnp.maximum(s - 1, 0), 0)),
        out_shape=jax.ShapeDtypeStruct((t_tok, d), _F32),
        scratch_shapes=[
            pltpu.VMEM((tile, d), _F32),
            pltpu.VMEM((nm, tile, _LANES), _BF16),
            pltpu.VMEM((nm, tile, _LANES), _F32),
            pltpu.VMEM((tile, _STATE_W), _F32),
            pltpu.VMEM((tile, _STATE_W), _F32),
            pltpu.VMEM((tile, _STATE_W), _BF16),
            pltpu.VMEM((tile, _STATE_W), _BF16),
            pltpu.VMEM((nm, _SUBLANES, _STATE_W), _F32),
            pltpu.VMEM((nm, _SUBLANES, _STATE_W), _F32),
            pltpu.VMEM((nm, 3 * _SUBLANES, _STATE_W), _F32),
            pltpu.VMEM((tile, d), _BF16),
            pltpu.VMEM((tile, d), _F32),
        ],
        compiler_params=_params(("arbitrary",)),
        name="s5_glu_layer",
    )(x2, gain.reshape(1, d).astype(_F32), d_skip.reshape(1, d).astype(_F32), lam8, wb, wc, perm, perm_t, w_glu)


def _ffn_kernel(x_ref, g_ref, fg_ref, win_hbm, wout_hbm, o_ref, hn_ref, wg_buf, wu_buf, wo_buf, sem,
                *, layer, hidden, tf, final_norm, row_split):
    i = pl.program_id(0)
    n_i = pl.num_programs(0)
    nf = hidden // tf
    assert nf % 2 == 0
    rows = o_ref.shape[0] // row_split
    slabs = [slice(s * rows, (s + 1) * rows) for s in range(row_split)]

    def copies(f, slot):
        return (
            pltpu.make_async_copy(win_hbm.at[layer, :, pl.ds(f * tf, tf)], wg_buf.at[slot], sem.at[0, slot]),
            pltpu.make_async_copy(win_hbm.at[layer, :, pl.ds(hidden + f * tf, tf)], wu_buf.at[slot],
                                  sem.at[1, slot]),
            pltpu.make_async_copy(wout_hbm.at[layer, pl.ds(f * tf, tf), :], wo_buf.at[slot], sem.at[2, slot]),
        )

    def start(f, slot):
        for cp in copies(f, slot):
            cp.start()

    def wait(f, slot):
        for cp in copies(f, slot):
            cp.wait()

    @pl.when(i == 0)
    def _first_fetch():
        start(0, 0)

    for sl in slabs:
        x = x_ref[sl, :]
        hn_ref[sl, :] = (x * _rms_scale(x) * g_ref[...]).astype(_BF16)
        o_ref[sl, :] = x

    def f_body(f, c):
        slot = f % 2
        wait(f, slot)

        @pl.when(f + 1 < nf)
        def _prefetch():
            start(f + 1, 1 - slot)

        @pl.when((f + 1 == nf) & (i + 1 < n_i))
        def _prefetch_next_row_tile():
            start(0, 0)

        wg = wg_buf[slot].astype(_BF16)
        wu = wu_buf[slot].astype(_BF16)
        wo = wo_buf[slot].astype(_BF16)
        for sl in slabs:
            hn = hn_ref[sl, :]
            a = jnp.dot(hn, wg, preferred_element_type=_F32)
            u = jnp.dot(hn, wu, preferred_element_type=_F32)
            act = (a * _sigmoid(a) * u).astype(_BF16)
            o_ref[sl, :] += jnp.dot(act, wo, preferred_element_type=_F32)
        return c

    lax.fori_loop(0, nf, f_body, 0)

    if final_norm:
        for sl in slabs:
            y = o_ref[sl, :]
            o_ref[sl, :] = y * _rms_scale(y) * fg_ref[...]


def _ffn_residual(x2, gain, w_in, w_out, layer, final_gain=None, *, tm=1024, tf=256, row_split=2):
    t_tok, d = x2.shape
    hidden = w_out.shape[1]
    final_norm = final_gain is not None
    fg = (final_gain if final_norm else gain).reshape(1, d).astype(_F32)
    return pl.pallas_call(
        functools.partial(_ffn_kernel, layer=layer, hidden=hidden, tf=tf, final_norm=final_norm,
                          row_split=row_split),
        grid=(t_tok // tm,),
        in_specs=[
            pl.BlockSpec((tm, d), lambda i: (i, 0)),
            pl.BlockSpec((1, d), lambda i: (0, 0)),
            pl.BlockSpec((1, d), lambda i: (0, 0)),
            pl.BlockSpec(memory_space=pl.ANY),
            pl.BlockSpec(memory_space=pl.ANY),
        ],
        out_specs=pl.BlockSpec((tm, d), lambda i: (i, 0)),
        out_shape=jax.ShapeDtypeStruct((t_tok, d), _F32),
        scratch_shapes=[
            pltpu.VMEM((tm, d), _BF16),
            pltpu.VMEM((2, d, tf), w_in.dtype),
            pltpu.VMEM((2, d, tf), w_in.dtype),
            pltpu.VMEM((2, tf, d), w_out.dtype),
            pltpu.SemaphoreType.DMA((3, 2)),
        ],
        compiler_params=_params(("arbitrary",)),
        name="ffn_residual",
    )(x2, gain.reshape(1, d).astype(_F32), fg, w_in, w_out)


_PERM_ROWS = 256


def _perm_matrix(dil, rows=_PERM_ROWS):
    dst = jnp.arange(rows)
    src = (dst % (rows // dil)) * dil + dst // (rows // dil)
    return (src[:, None] == jnp.arange(rows)[None, :]).astype(_BF16)


def _to_residue_major(y, p_ref, dil, store):
    tm = y.shape[0]
    run = _PERM_ROWS // dil
    for j in range(tm // _PERM_ROWS):
        yp = jnp.dot(p_ref[...], y[j * _PERM_ROWS:(j + 1) * _PERM_ROWS], preferred_element_type=_F32).astype(_BF16)
        for r in range(dil):
            store(r, j * run, yp[r * run:(r + 1) * run])


def _qkv_kernel(*refs, dils, nj):
    n_groups = len(dils)
    n_p = sum(1 for dl in dils if dl > 1)
    x_ref, gq_ref, gkv_ref, w_ref = refs[:4]
    p_refs = refs[4:4 + n_p]
    q_refs = refs[4 + n_p:4 + n_p + n_groups]
    kv_refs = refs[4 + n_p + n_groups:4 + n_p + 2 * n_groups]
    hn_ref = refs[-1]
    tm = hn_ref.shape[1]
    p_of = {}
    for dl in dils:
        if dl > 1:
            p_of[dl] = p_refs[len(p_of)]
    j = pl.program_id(2)

    @pl.when(j == 0)
    def _start():
        for sb in range(tm // _PERM_ROWS):
            rows = slice(sb * _PERM_ROWS, (sb + 1) * _PERM_ROWS)
            x = x_ref[rows, :]
            xr = x * _rms_scale(x)
            hq = (xr * gq_ref[...]).astype(_BF16)
            hn_ref[n_groups, rows, :] = (xr * gkv_ref[...]).astype(_BF16)
            for g, dl in enumerate(dils):
                if dl == 1:
                    hn_ref[g, rows, :] = hq
                else:
                    run = _PERM_ROWS // dl
                    yp = jnp.dot(p_of[dl][...], hq, preferred_element_type=_F32).astype(_BF16)
                    for r in range(dl):
                        dst = r * (tm // dl) + sb * run
                        hn_ref[g, dst:dst + run, :] = yp[r * run:(r + 1) * run]

    for g, dl in enumerate(dils):
        @pl.when((j >= g * nj) & (j < (g + 1) * nj))
        def _q_tile(g=g, dl=dl):
            res = jnp.dot(hn_ref[g], w_ref[...], preferred_element_type=_F32).astype(_BF16)
            for r in range(dl):
                q_refs[g][r] = res[r * (tm // dl):(r + 1) * (tm // dl)]

    @pl.when(j >= n_groups * nj)
    def _kv_tile():
        kv = jnp.dot(hn_ref[n_groups], w_ref[...], preferred_element_type=_F32).astype(_BF16)
        for g, dl in enumerate(dils):
            if dl == 1:
                kv_refs[g][0] = kv
            else:
                def store(r, row, block, g=g):
                    kv_refs[g][r, row:row + block.shape[0], :] = block
                _to_residue_major(kv, p_of[dl], dl, store)


def _qkv_project(x2, bsz, seq, q_gain, kv_gain, w_cat, qw, dils, *, tm=1024, tn=512):
    t_tok, d = x2.shape
    n_groups = len(dils)
    kvw = w_cat.shape[1] - n_groups * qw
    assert qw % tn == 0 and kvw % tn == 0
    nj = qw // tn
    nkv = kvw // tn
    nt = seq // tm
    perms = [_perm_matrix(dl) for dl in dils if dl > 1]

    def q_spec(g, dl):
        return pl.BlockSpec((None, dl, tm // dl, tn),
                            lambda b, t, j: (b, 0, t, jnp.clip(j - g * nj, 0, nj - 1)))

    outs = pl.pallas_call(
        functools.partial(_qkv_kernel, dils=tuple(dils), nj=nj),
        grid=(bsz, nt, n_groups * nj + nkv),
        in_specs=([pl.BlockSpec((tm, d), lambda b, t, j: (b * nt + t, 0)),
                   pl.BlockSpec((1, d), lambda b, t, j: (0, 0)),
                   pl.BlockSpec((1, d), lambda b, t, j: (0, 0)),
                   pl.BlockSpec((d, tn), lambda b, t, j: (0, j))]
                  + [pl.BlockSpec((_PERM_ROWS, _PERM_ROWS), lambda b, t, j: (0, 0))] * len(perms)),
        out_specs=([q_spec(g, dl) for g, dl in enumerate(dils)]
                   + [pl.BlockSpec((None, dl, tm // dl, tn),
                                   lambda b, t, j: (b, 0, t, jnp.clip(j - n_groups * nj, 0, nkv - 1)))
                      for dl in dils]),
        out_shape=([jax.ShapeDtypeStruct((bsz, dl, seq // dl, qw), _BF16) for dl in dils]
                   + [jax.ShapeDtypeStruct((bsz, dl, seq // dl, kvw), _BF16) for dl in dils]),
        scratch_shapes=[pltpu.VMEM((n_groups + 1, tm, d), _BF16)],
        compiler_params=_params(("parallel", "parallel", "arbitrary")),
        name="qkv_project",
    )(x2, q_gain.reshape(1, d).astype(_F32), kv_gain.reshape(1, d).astype(_F32), w_cat, *perms)
    return outs[:n_groups], outs[n_groups:]


def _attn_kernel(q_ref, kp_ref, kc_ref, vp_ref, vc_ref, o_ref, st_ref, *, blk, rep, scale):
    i = pl.program_id(2)
    nq = q_ref.shape[0] // blk
    qi = lax.broadcasted_iota(jnp.int32, (blk, 2 * blk), 0)
    si = lax.broadcasted_iota(jnp.int32, (blk, 2 * blk), 1)
    dist = qi + blk - si
    band = (dist >= 0) & (dist <= blk)
    band_first = band & ((i > 0) | (si >= blk))
    lane = lax.broadcasted_iota(jnp.int32, (blk, _LANES), 1)
    n_kv = kc_ref.shape[1] // _HEAD_DIM
    for sub in range(nq):
        valid = jnp.concatenate([band_first if sub == 0 else band] * rep, axis=0)
        rows = slice(sub * blk, (sub + 1) * blk)
        st = jnp.zeros((blk, _LANES), _F32)
        for kvh in range(n_kv):
            ks = slice(kvh * _HEAD_DIM, (kvh + 1) * _HEAD_DIM)
            if sub == 0:
                k = jnp.concatenate([kp_ref[:, ks], kc_ref[0:blk, ks]], axis=0)
                v = jnp.concatenate([vp_ref[:, ks], vc_ref[0:blk, ks]], axis=0)
            else:
                k = kc_ref[(sub - 1) * blk:(sub + 1) * blk, ks]
                v = vc_ref[(sub - 1) * blk:(sub + 1) * blk, ks]
            q = jnp.concatenate(
                [q_ref[rows, (kvh * rep + rp) * _HEAD_DIM:(kvh * rep + rp + 1) * _HEAD_DIM] for rp in range(rep)],
                axis=0)
            s = lax.dot_general(q, k, (((1,), (1,)), ((), ())), preferred_element_type=_F32) * scale
            s = jnp.where(valid, s, _NEG_INF)
            m = jnp.max(s, axis=-1, keepdims=True)
            p = jnp.exp(s - m)
            l = jnp.sum(p, axis=-1, keepdims=True)
            pn = (p * (1.0 / l)).astype(_BF16)
            o = jnp.dot(pn, v, preferred_element_type=_F32)
            lse = m + jnp.log(l)
            for rp in range(rep):
                hd = kvh * rep + rp
                o_ref[rows, hd * _HEAD_DIM:(hd + 1) * _HEAD_DIM] = o[rp * blk:(rp + 1) * blk].astype(o_ref.dtype)
                st = jnp.where(lane == hd, lse[rp * blk:(rp + 1) * blk], st)
        st_ref[rows, :] = st


def _dilated_attention(q, kv, window, dilation, *, nq=2):
    bsz, dil, n, qw = q.shape
    assert dil == dilation
    kw = kv.shape[3] // 2
    n_heads = qw // _HEAD_DIM
    rep = n_heads // (kw // _HEAD_DIM)
    blk = window // dilation
    assert n % (nq * blk) == 0 and blk == _LANES and n_heads <= _LANES
    nb = n // (nq * blk)
    prev = lambda i: jnp.maximum(nq * i - 1, 0)
    return pl.pallas_call(
        functools.partial(_attn_kernel, blk=blk, rep=rep, scale=_HEAD_DIM ** -0.5),
        grid=(bsz, dilation, nb),
        in_specs=[
            pl.BlockSpec((None, None, nq * blk, qw), lambda b, r, i: (b, r, i, 0)),
            pl.BlockSpec((None, None, blk, kw), lambda b, r, i: (b, r, prev(i), 0)),
            pl.BlockSpec((None, None, nq * blk, kw), lambda b, r, i: (b, r, i, 0)),
            pl.BlockSpec((None, None, blk, kw), lambda b, r, i: (b, r, prev(i), 1)),
            pl.BlockSpec((None, None, nq * blk, kw), lambda b, r, i: (b, r, i, 1)),
        ],
        out_specs=[
            pl.BlockSpec((None, None, nq * blk, qw), lambda b, r, i: (b, r, i, 0)),
            pl.BlockSpec((None, None, nq * blk, _LANES), lambda b, r, i: (b, r, i, 0)),
        ],
        out_shape=[
            jax.ShapeDtypeStruct((bsz, dilation, n, qw), _BF16),
            jax.ShapeDtypeStruct((bsz, dilation, n, _LANES), _F32),
        ],
        compiler_params=_params(("parallel", "parallel", "arbitrary")),
        name=f"dilated_attention_d{dilation}",
    )(q, kv, kv, kv, kv)


def _combine_kernel(*refs, dils, n_heads):
    n_groups = len(dils)
    n_p = sum(1 for dl in dils if dl > 1)
    o_in = refs[:n_groups]
    s_in = refs[n_groups:2 * n_groups]
    x_ref, wo_ref = refs[2 * n_groups:2 * n_groups + 2]
    pt_refs = refs[2 * n_groups + 2:2 * n_groups + 2 + n_p]
    out_ref, onat_ref, snat_ref, comb_ref = refs[2 * n_groups + 2 + n_p:]
    tm = x_ref.shape[0]

    p_of = {}
    for dl in dils:
        if dl > 1:
            p_of[dl] = pt_refs[len(p_of)]

    for j in range(tm // _PERM_ROWS):
        rs = slice(j * _PERM_ROWS, (j + 1) * _PERM_ROWS)
        for g, dl in enumerate(dils):
            if dl == 1:
                onat_ref[g, rs, :] = o_in[g][0, rs, :]
                snat_ref[g, rs, :] = s_in[g][0, rs, :]
                continue
            run = _PERM_ROWS // dl
            pt = p_of[dl][...]
            oc = jnp.concatenate([o_in[g][r, j * run:(j + 1) * run, :] for r in range(dl)], axis=0)
            onat_ref[g, rs, :] = jnp.dot(pt, oc, preferred_element_type=_F32).astype(_BF16)
            sc = jnp.concatenate([s_in[g][r, j * run:(j + 1) * run, :] for r in range(dl)], axis=0)
            s_hi = sc.astype(_BF16)
            rem = sc - s_hi.astype(_F32)
            s_mid = rem.astype(_BF16)
            s_lo = (rem - s_mid.astype(_F32)).astype(_BF16)
            snat_ref[g, rs, :] = (jnp.dot(pt, s_hi, preferred_element_type=_F32)
                                  + jnp.dot(pt, s_mid, preferred_element_type=_F32)
                                  + jnp.dot(pt, s_lo, preferred_element_type=_F32))

        lses = [snat_ref[g, rs, :] for g in range(n_groups)]
        mx = lses[0]
        for l in lses[1:]:
            mx = jnp.maximum(mx, l)
        es = [jnp.exp(l - mx) for l in lses]
        den = es[0]
        for e in es[1:]:
            den = den + e
        inv = 1.0 / den
        wts = [e * inv for e in es]
        for hd in range(n_heads):
            cs = slice(hd * _HEAD_DIM, (hd + 1) * _HEAD_DIM)
            base = onat_ref[0, rs, cs].astype(_F32)
            acc = base
            for g in range(1, n_groups):
                acc = acc + wts[g][:, hd:hd + 1] * (onat_ref[g, rs, cs].astype(_F32) - base)
            comb_ref[rs, cs] = acc.astype(_BF16)
        out_ref[rs, :] = x_ref[rs, :] + jnp.dot(comb_ref[rs, :], wo_ref[...], preferred_element_type=_F32)


def _combine_project(outs, stats, x2, w_o, layer, dils, *, tm=512):
    t_tok, d = x2.shape
    bsz, _, _, qw = outs[0].shape
    seq = t_tok // bsz
    nt = seq // tm
    n_groups = len(outs)
    perms_t = [_perm_matrix(dl).T for dl in dils if dl > 1]
    grp = lambda b, t: (b, 0, t, 0)
    return pl.pallas_call(
        functools.partial(_combine_kernel, dils=tuple(dils), n_heads=qw // _HEAD_DIM),
        grid=(bsz, nt),
        in_specs=([pl.BlockSpec((None, dl, tm // dl, qw), grp) for dl in dils]
                  + [pl.BlockSpec((None, dl, tm // dl, _LANES), grp) for dl in dils]
                  + [pl.BlockSpec((tm, d), lambda b, t: (b * nt + t, 0)),
                     pl.BlockSpec((None, qw, d), lambda b, t: (layer, 0, 0))]
                  + [pl.BlockSpec((_PERM_ROWS, _PERM_ROWS), lambda b, t: (0, 0))] * len(perms_t)),
        out_specs=pl.BlockSpec((tm, d), lambda b, t: (b * nt + t, 0)),
        out_shape=jax.ShapeDtypeStruct((t_tok, d), _F32),
        scratch_shapes=[pltpu.VMEM((n_groups, tm, qw), _BF16),
                        pltpu.VMEM((n_groups, tm, _LANES), _F32),
                        pltpu.VMEM((tm, qw), _BF16)],
        compiler_params=_params(("parallel", "arbitrary")),
        name="combine_project",
    )(*outs, *stats, x2, w_o, *perms_t)


def kernel(x, s5_lam_re, s5_lam_im, s5_log_dt, s5_b_re, s5_b_im, s5_c_re, s5_c_im, s5_d, s5_w_glu, a_norm_mix, ffn_norm, ffn_w_in, ffn_w_out, b_norm_mix, attn_w_q, attn_w_o, kv_norm, w_kv, final_norm):
    bsz, seq, d = x.shape
    n_a = a_norm_mix.shape[0]
    n_b = b_norm_mix.shape[0]
    depth = n_a + n_b
    n_groups = len(_DILATED_PATTERNS)
    dils = [dl for _, dl in _DILATED_PATTERNS]
    x2 = x.reshape(bsz * seq, d).astype(_F32)
    w_o_b = attn_w_o.astype(_BF16)
    assert n_b == 1, "the fused q/kv projection covers the single attention layer of this model"
    qw = attn_w_q.shape[2] // n_groups
    for layer in range(depth):
        if layer < n_a:
            i = layer
            lam8, wb, wc = _s5_tables(s5_lam_re[i], s5_lam_im[i], s5_log_dt[i], s5_b_re[i], s5_b_im[i],
                                      s5_c_re[i], s5_c_im[i])
            x2 = _s5_glu_layer(x2, bsz, seq, a_norm_mix[i], s5_d[i], lam8, wb, wc, s5_w_glu[i].astype(_BF16))
        else:
            j = layer - n_a
            w_cat = jnp.concatenate([attn_w_q[j], w_kv], axis=1).astype(_BF16)
            qs, kvs = _qkv_project(x2, bsz, seq, b_norm_mix[j], kv_norm, w_cat, qw, dils)
            outs, stats = [], []
            for g, (window, dilation) in enumerate(_DILATED_PATTERNS):
                o, st = _dilated_attention(qs[g], kvs[g], window, dilation)
                outs.append(o)
                stats.append(st)
            x2 = _combine_project(outs, stats, x2, w_o_b, j, dils)
        final_gain = final_norm if layer == depth - 1 else None
        x2 = _ffn_residual(x2, ffn_norm[layer], ffn_w_in, ffn_w_out, layer, final_gain)
    return x2.reshape(bsz, seq, d).astype(x.dtype)
```

```python
import functools
import math

import jax
import jax.numpy as jnp
from jax import lax
from jax.experimental import pallas as pl
from jax.experimental.pallas import tpu as pltpu

_F32 = jnp.float32
_BF16 = jnp.bfloat16
_EPS = 1e-6
_NEG_INF = -1e30

_LANES = 128
_SUBLANES = 8
_VMEM_LIMIT_BYTES = 60 * 1024 * 1024

_S5_GROUP_CH = 16
_S5_STATE = 64
_HEAD_DIM = 128
_N_KV_HEADS = 4
_DILATED_PATTERNS = ((128, 1), (512, 4), (2048, 16))
_GROUPS_PER_TILE = _LANES // _S5_GROUP_CH
_HALF = _GROUPS_PER_TILE * _S5_STATE
_STATE_W = 2 * _HALF


def _params(sem):
    return pltpu.CompilerParams(dimension_semantics=sem, vmem_limit_bytes=_VMEM_LIMIT_BYTES)


def _rms_scale(x):
    return lax.rsqrt(jnp.mean(x * x, axis=-1, keepdims=True) + _EPS)


def _sigmoid(v):
    return 1.0 / (1.0 + jnp.exp(-v))


def _cmul(ar, ai, br, bi):
    return ar * br - ai * bi, ar * bi + ai * br


def _s5_kernel(x_ref, g_ref, dsk_ref, lam_ref, wb_ref, wc_ref, p_ref, pt_ref, wglu_ref, o_ref,
               h_ref, hbp_ref, y_ref, bu0_ref, bu1_ref, xb0_ref, xb1_ref, carry_ref, cp_ref, lvl_ref,
               zprev_ref, xprev_ref, *, lc, tiles_per_seq):
    s = pl.program_id(0)
    n_tiles = lam_ref.shape[0]
    d_model = o_ref.shape[1]
    n_glu = n_tiles // 2
    glu_w = d_model // n_glu
    rows = lax.broadcasted_iota(jnp.int32, (_SUBLANES, _HALF), 0)

    @pl.when(s % tiles_per_seq == 0)
    def _new_sequence():
        carry_ref[...] = jnp.zeros_like(carry_ref)

    @pl.when(s == 0)
    def _init():
        zprev_ref[...] = jnp.zeros_like(zprev_ref)
        xprev_ref[...] = jnp.zeros_like(xprev_ref)

        def init_m(m, c):
            pr = lam_ref[m, :, 0:_HALF]
            pi = lam_ref[m, :, _HALF:_STATE_W]
            for _ in range(int(math.log2(lc))):
                pr, pi = _cmul(pr, pi, pr, pi)
            qr, qi = pr, pi
            for lv in range(3):
                lvl_ref[m, lv * _SUBLANES:(lv + 1) * _SUBLANES, 0:_HALF] = qr
                lvl_ref[m, lv * _SUBLANES:(lv + 1) * _SUBLANES, _HALF:_STATE_W] = qi
                qr, qi = _cmul(qr, qi, qr, qi)
            cr, ci = pr, pi
            outr, outi = pr, pi
            for c_idx in range(1, _SUBLANES):
                cr, ci = _cmul(cr, ci, pr, pi)
                outr = jnp.where(rows == c_idx, cr, outr)
                outi = jnp.where(rows == c_idx, ci, outi)
            cp_ref[m, :, 0:_HALF] = outr
            cp_ref[m, :, _HALF:_STATE_W] = outi
            return c

        lax.fori_loop(0, n_tiles, init_m, 0)

    x = x_ref[...]
    h = x * _rms_scale(x) * g_ref[...]
    h_ref[...] = h
    hp = jnp.dot(p_ref[...], h.astype(_BF16), preferred_element_type=_F32)
    for m in range(n_tiles):
        hbp_ref[m] = hp[:, m * _LANES:(m + 1) * _LANES].astype(_BF16)

    def project_in(m, bu_ref):
        bu_ref[...] = jnp.dot(hbp_ref[m], wb_ref[m], preferred_element_type=_F32)

    def project_out(m, xb_ref):
        y_ref[m] = jnp.dot(xb_ref[...], wc_ref[m], preferred_element_type=_F32)

    def scan(m, bu_ref, xb_ref):
        lr = lam_ref[m, :, 0:_HALF]
        li = lam_ref[m, :, _HALF:_STATE_W]

        xr = jnp.zeros((_SUBLANES, _HALF), _F32)
        xi = jnp.zeros((_SUBLANES, _HALF), _F32)
        for tau in range(lc):
            r0 = tau * _SUBLANES
            tr, ti = _cmul(lr, li, xr, xi)
            xr = tr + bu_ref[r0:r0 + _SUBLANES, 0:_HALF]
            xi = ti + bu_ref[r0:r0 + _SUBLANES, _HALF:_STATE_W]
            bu_ref[r0:r0 + _SUBLANES, 0:_HALF] = xr
            bu_ref[r0:r0 + _SUBLANES, _HALF:_STATE_W] = xi

        zr, zi = xr, xi
        for lv, s in enumerate((1, 2, 4)):
            ar = lvl_ref[m, lv * _SUBLANES:(lv + 1) * _SUBLANES, 0:_HALF]
            ai = lvl_ref[m, lv * _SUBLANES:(lv + 1) * _SUBLANES, _HALF:_STATE_W]
            sr = jnp.where(rows >= s, pltpu.roll(zr, s, 0), 0.0)
            si = jnp.where(rows >= s, pltpu.roll(zi, s, 0), 0.0)
            tr, ti = _cmul(ar, ai, sr, si)
            zr = zr + tr
            zi = zi + ti
        pr = carry_ref[m, :, 0:_HALF]
        pi = carry_ref[m, :, _HALF:_STATE_W]
        tr, ti = _cmul(cp_ref[m, :, 0:_HALF], cp_ref[m, :, _HALF:_STATE_W], pr, pi)
        er = zr + tr
        ei = zi + ti
        cr = jnp.where(rows >= 1, pltpu.roll(er, 1, 0), pr)
        ci = jnp.where(rows >= 1, pltpu.roll(ei, 1, 0), pi)
        carry_ref[m, :, 0:_HALF] = jnp.broadcast_to(er[_SUBLANES - 1:_SUBLANES, :], (_SUBLANES, _HALF))
        carry_ref[m, :, _HALF:_STATE_W] = jnp.broadcast_to(ei[_SUBLANES - 1:_SUBLANES, :], (_SUBLANES, _HALF))

        for tau in range(0, lc, 2):
            r0 = tau * _SUBLANES
            cr, ci = _cmul(lr, li, cr, ci)
            x0r = bu_ref[r0:r0 + _SUBLANES, 0:_HALF] + cr
            x0i = bu_ref[r0:r0 + _SUBLANES, _HALF:_STATE_W] + ci
            cr, ci = _cmul(lr, li, cr, ci)
            x1r = bu_ref[r0 + _SUBLANES:r0 + 2 * _SUBLANES, 0:_HALF] + cr
            x1i = bu_ref[r0 + _SUBLANES:r0 + 2 * _SUBLANES, _HALF:_STATE_W] + ci
            xb_ref[r0:r0 + 2 * _SUBLANES, 0:_HALF] = jnp.concatenate([x0r, x1r], axis=0).astype(_BF16)
            xb_ref[r0:r0 + 2 * _SUBLANES, _HALF:_STATE_W] = jnp.concatenate([x0i, x1i], axis=0).astype(_BF16)

    def glu_value(k):
        cs = slice(k * glu_w, (k + 1) * glu_w)
        o_ref[:, cs] = jnp.dot(zprev_ref[...], wglu_ref[:, cs], preferred_element_type=_F32)

    def glu_gate(k):
        cs = slice(k * glu_w, (k + 1) * glu_w)
        gate = jnp.dot(zprev_ref[...], wglu_ref[:, d_model + k * glu_w:d_model + (k + 1) * glu_w],
                       preferred_element_type=_F32)
        o_ref[:, cs] = xprev_ref[:, cs] + o_ref[:, cs] * _sigmoid(gate)

    project_in(0, bu0_ref)
    for k in range(n_glu):
        m0 = 2 * k
        project_in(m0 + 1, bu1_ref)
        if k > 0:
            project_out(m0 - 1, xb1_ref)
        glu_value(k)
        scan(m0, bu0_ref, xb0_ref)
        if m0 + 2 < n_tiles:
            project_in(m0 + 2, bu0_ref)
        project_out(m0, xb0_ref)
        glu_gate(k)
        scan(m0 + 1, bu1_ref, xb1_ref)
    project_out(n_tiles - 1, xb1_ref)

    y = jnp.concatenate([y_ref[m] for m in range(n_tiles)], axis=1)
    y_hi = y.astype(_BF16)
    y_lo = (y - y_hi.astype(_F32)).astype(_BF16)
    pt = pt_ref[...]
    yn = (jnp.dot(pt, y_hi, preferred_element_type=_F32)
          + jnp.dot(pt, y_lo, preferred_element_type=_F32))
    yn = yn + dsk_ref[...] * h_ref[...]
    cdf = 0.5 * (1.0 + jnp.tanh(math.sqrt(2.0 / math.pi) * (yn + 0.044715 * (yn * yn * yn))))
    zprev_ref[...] = (yn * cdf).astype(zprev_ref.dtype)
    xprev_ref[...] = x_ref[...]


def _s5_tables(lam_re, lam_im, log_dt, b_re, b_im, c_re, c_im):
    g, p = lam_re.shape
    c = b_re.shape[-1]
    nm = g // _GROUPS_PER_TILE
    lr = lam_re.astype(_F32)
    li = lam_im.astype(_F32)
    dt = jnp.exp(log_dt.astype(_F32))[:, None]
    mag = jnp.exp(lr * dt)
    ang = li * dt
    lb_re = mag * jnp.cos(ang)
    lb_im = mag * jnp.sin(ang)
    nr = lb_re - 1.0
    den = lr * lr + li * li
    f_re = (nr * lr + lb_im * li) / den
    f_im = (lb_im * lr - nr * li) / den
    br = b_re.astype(_F32)
    bi = b_im.astype(_F32)
    bb_re = f_re[..., None] * br - f_im[..., None] * bi
    bb_im = f_re[..., None] * bi + f_im[..., None] * br
    eye = jnp.eye(_GROUPS_PER_TILE, dtype=_F32)

    def blk_b(bb):
        t = bb.reshape(nm, _GROUPS_PER_TILE, p, c)
        return jnp.einsum('mgpc,gh->mgchp', t, eye).reshape(nm, _GROUPS_PER_TILE * c, _GROUPS_PER_TILE * p)

    def blk_c(cc):
        t = cc.reshape(nm, _GROUPS_PER_TILE, c, p)
        return jnp.einsum('mgcp,gh->mgphc', t, eye).reshape(nm, _GROUPS_PER_TILE * p, _GROUPS_PER_TILE * c)

    wb = jnp.concatenate([blk_b(bb_re), blk_b(bb_im)], axis=-1).astype(_BF16)
    wc = jnp.concatenate([blk_c(c_re.astype(_F32)), blk_c(-c_im.astype(_F32))], axis=1).astype(_BF16)
    lam = jnp.concatenate([lb_re.reshape(nm, _HALF), lb_im.reshape(nm, _HALF)], axis=-1)
    lam8 = jnp.broadcast_to(lam[:, None, :], (nm, _SUBLANES, _STATE_W))
    return lam8, wb, wc


def _s5_glu_layer(x2, bsz, seq, gain, d_skip, lam8, wb, wc, w_glu, *, tile=256):
    t_tok, d = x2.shape
    nm = lam8.shape[0]
    lc = tile // _SUBLANES
    nt = seq // tile
    n_steps = bsz * nt
    assert (d // (nm // 2)) % _LANES == 0
    r = jnp.arange(tile)
    col = (r % _SUBLANES) * lc + r // _SUBLANES
    perm = (col[:, None] == jnp.arange(tile)[None, :]).astype(_BF16)
    perm_t = perm.T
    const3 = lambda s: (0, 0, 0)
    const2 = lambda s: (0, 0)
    once = dict(pipeline_mode=pl.Buffered(1))
    return pl.pallas_call(
        functools.partial(_s5_kernel, lc=lc, tiles_per_seq=nt),
        grid=(n_steps + 1,),
        in_specs=[
            pl.BlockSpec((tile, d), lambda s: (jnp.minimum(s, n_steps - 1), 0)),
            pl.BlockSpec((1, d), const2),
            pl.BlockSpec((1, d), const2),
            pl.BlockSpec((nm, _SUBLANES, _STATE_W), const3, **once),
            pl.BlockSpec((nm, _LANES, _STATE_W), const3, **once),
            pl.BlockSpec((nm, _STATE_W, _LANES), const3, **once),
            pl.BlockSpec((tile, tile), const2),
            pl.BlockSpec((tile, tile), const2),
            pl.BlockSpec((d, 2 * d), const2, **once),
        ],
        out_specs=pl.BlockSpec((tile, d), lambda s: (jnp.maximum(s - 1, 0), 0)),
        out_shape=jax.ShapeDtypeStruct((t_tok, d), _F32),
        scratch_shapes=[
            pltpu.VMEM((tile, d), _F32),
            pltpu.VMEM((nm, tile, _LANES), _BF16),
            pltpu.VMEM((nm, tile, _LANES), _F32),
            pltpu.VMEM((tile, _STATE_W), _F32),
            pltpu.VMEM((tile, _STATE_W), _F32),
            pltpu.VMEM((tile, _STATE_W), _BF16),
            pltpu.VMEM((tile, _STATE_W), _BF16),
            pltpu.VMEM((nm, _SUBLANES, _STATE_W), _F32),
            pltpu.VMEM((nm, _SUBLANES, _STATE_W), _F32),
            pltpu.VMEM((nm, 3 * _SUBLANES, _STATE_W), _F32),
            pltpu.VMEM((tile, d), _BF16),
            pltpu.VMEM((tile, d), _F32),
        ],
        compiler_params=_params(("arbitrary",)),
        name="s5_glu_layer",
    )(x2, gain.reshape(1, d).astype(_F32), d_skip.reshape(1, d).astype(_F32), lam8, wb, wc, perm, perm_t, w_glu)


def _ffn_kernel(x_ref, g_ref, fg_ref, win_hbm, wout_hbm, o_ref, hn_ref, wg_buf, wu_buf, wo_buf, sem,
                *, layer, hidden, tf, final_norm, row_split):
    i = pl.program_id(0)
    n_i = pl.num_programs(0)
    nf = hidden // tf
    rows = o_ref.shape[0] // row_split
    slabs = [slice(s * rows, (s + 1) * rows) for s in range(row_split)]

    def copies(f, slot):
        return (
            pltpu.make_async_copy(win_hbm.at[layer, :, pl.ds(f * tf, tf)], wg_buf.at[slot], sem.at[0, slot]),
            pltpu.make_async_copy(win_hbm.at[layer, :, pl.ds(hidden + f * tf, tf)], wu_buf.at[slot],
                                  sem.at[1, slot]),
            pltpu.make_async_copy(wout_hbm.at[layer, pl.ds(f * tf, tf), :], wo_buf.at[slot], sem.at[2, slot]),
        )

    def start(f, slot):
        for cp in copies(f, slot):
            cp.start()

    def wait(f, slot):
        for cp in copies(f, slot):
            cp.wait()

    n_slots = wg_buf.shape[0]
    ahead = n_slots - 1
    total = n_i * nf

    @pl.when(i == 0)
    def _first_fetches():
        for g0 in range(ahead):
            start(g0 % nf, g0 % n_slots)

    for sl in slabs:
        x = x_ref[sl, :]
        hn_ref[sl, :] = (x * _rms_scale(x) * g_ref[...]).astype(_BF16)
        o_ref[sl, :] = x

    def f_body(f, c):
        g = i * nf + f
        slot = lax.rem(g, n_slots)
        wait(f, slot)

        @pl.when(g + ahead < total)
        def _prefetch():
            start(lax.rem(f + ahead, nf), lax.rem(g + ahead, n_slots))

        wg = wg_buf[slot].astype(_BF16)
        wu = wu_buf[slot].astype(_BF16)
        wo = wo_buf[slot].astype(_BF16)
        for sl in slabs:
            hn = hn_ref[sl, :]
            a = jnp.dot(hn, wg, preferred_element_type=_F32)
            u = jnp.dot(hn, wu, preferred_element_type=_F32)
            act = (a * _sigmoid(a) * u).astype(_BF16)
            o_ref[sl, :] += jnp.dot(act, wo, preferred_element_type=_F32)
        return c

    lax.fori_loop(0, nf, f_body, 0)

    if final_norm:
        for sl in slabs:
            y = o_ref[sl, :]
            o_ref[sl, :] = y * _rms_scale(y) * fg_ref[...]


def _ffn_residual(x2, gain, w_in, w_out, layer, final_gain=None, *, tm=1024, tf=256, row_split=2, n_slots=3):
    t_tok, d = x2.shape
    hidden = w_out.shape[1]
    final_norm = final_gain is not None
    fg = (final_gain if final_norm else gain).reshape(1, d).astype(_F32)
    return pl.pallas_call(
        functools.partial(_ffn_kernel, layer=layer, hidden=hidden, tf=tf, final_norm=final_norm,
                          row_split=row_split),
        grid=(t_tok // tm,),
        in_specs=[
            pl.BlockSpec((tm, d), lambda i: (i, 0)),
            pl.BlockSpec((1, d), lambda i: (0, 0)),
            pl.BlockSpec((1, d), lambda i: (0, 0)),
            pl.BlockSpec(memory_space=pl.ANY),
            pl.BlockSpec(memory_space=pl.ANY),
        ],
        out_specs=pl.BlockSpec((tm, d), lambda i: (i, 0)),
        out_shape=jax.ShapeDtypeStruct((t_tok, d), _F32),
        scratch_shapes=[
            pltpu.VMEM((tm, d), _BF16),
            pltpu.VMEM((n_slots, d, tf), w_in.dtype),
            pltpu.VMEM((n_slots, d, tf), w_in.dtype),
            pltpu.VMEM((n_slots, tf, d), w_out.dtype),
            pltpu.SemaphoreType.DMA((3, n_slots)),
        ],
        compiler_params=_params(("arbitrary",)),
        name="ffn_residual",
    )(x2, gain.reshape(1, d).astype(_F32), fg, w_in, w_out)


_PERM_ROWS = 256


def _perm_matrix(dil, rows=_PERM_ROWS):
    dst = jnp.arange(rows)
    src = (dst % (rows // dil)) * dil + dst // (rows // dil)
    return (src[:, None] == jnp.arange(rows)[None, :]).astype(_BF16)


def _to_residue_major(y, p_ref, dil, store):
    tm = y.shape[0]
    run = _PERM_ROWS // dil
    for j in range(tm // _PERM_ROWS):
        yp = jnp.dot(p_ref[...], y[j * _PERM_ROWS:(j + 1) * _PERM_ROWS], preferred_element_type=_F32).astype(_BF16)
        for r in range(dil):
            store(r, j * run, yp[r * run:(r + 1) * run])


def _qkv_kernel(*refs, dils, nj):
    n_groups = len(dils)
    n_p = sum(1 for dl in dils if dl > 1)
    x_ref, gq_ref, gkv_ref, w_ref = refs[:4]
    p_refs = refs[4:4 + n_p]
    q_refs = refs[4 + n_p:4 + n_p + n_groups]
    kv_refs = refs[4 + n_p + n_groups:4 + n_p + 2 * n_groups]
    hn_ref = refs[-1]
    tm = hn_ref.shape[1]
    p_of = {}
    for dl in dils:
        if dl > 1:
            p_of[dl] = p_refs[len(p_of)]
    j = pl.program_id(2)

    @pl.when(j == 0)
    def _start():
        for sb in range(tm // _PERM_ROWS):
            rows = slice(sb * _PERM_ROWS, (sb + 1) * _PERM_ROWS)
            x = x_ref[rows, :]
            xr = x * _rms_scale(x)
            hq = (xr * gq_ref[...]).astype(_BF16)
            hn_ref[n_groups, rows, :] = (xr * gkv_ref[...]).astype(_BF16)
            for g, dl in enumerate(dils):
                if dl == 1:
                    hn_ref[g, rows, :] = hq
                else:
                    run = _PERM_ROWS // dl
                    yp = jnp.dot(p_of[dl][...], hq, preferred_element_type=_F32).astype(_BF16)
                    for r in range(dl):
                        dst = r * (tm // dl) + sb * run
                        hn_ref[g, dst:dst + run, :] = yp[r * run:(r + 1) * run]

    for g, dl in enumerate(dils):
        @pl.when((j >= g * nj) & (j < (g + 1) * nj))
        def _q_tile(g=g, dl=dl):
            res = jnp.dot(hn_ref[g], w_ref[...], preferred_element_type=_F32).astype(_BF16)
            for r in range(dl):
                q_refs[g][r] = res[r * (tm // dl):(r + 1) * (tm // dl)]

    @pl.when(j >= n_groups * nj)
    def _kv_tile():
        kv = jnp.dot(hn_ref[n_groups], w_ref[...], preferred_element_type=_F32).astype(_BF16)
        for g, dl in enumerate(dils):
            if dl == 1:
                kv_refs[g][0] = kv
            else:
                def store(r, row, block, g=g):
                    kv_refs[g][r, row:row + block.shape[0], :] = block
                _to_residue_major(kv, p_of[dl], dl, store)


def _qkv_project(x2, bsz, seq, q_gain, kv_gain, w_cat, qw, dils, *, tm=1024, tn=512):
    t_tok, d = x2.shape
    n_groups = len(dils)
    kvw = w_cat.shape[1] - n_groups * qw
    assert qw % tn == 0 and kvw % tn == 0
    nj = qw // tn
    nkv = kvw // tn
    nt = seq // tm
    perms = [_perm_matrix(dl) for dl in dils if dl > 1]

    def q_spec(g, dl):
        return pl.BlockSpec((None, dl, tm // dl, tn),
                            lambda b, t, j: (b, 0, t, jnp.clip(j - g * nj, 0, nj - 1)))

    outs = pl.pallas_call(
        functools.partial(_qkv_kernel, dils=tuple(dils), nj=nj),
        grid=(bsz, nt, n_groups * nj + nkv),
        in_specs=([pl.BlockSpec((tm, d), lambda b, t, j: (b * nt + t, 0)),
                   pl.BlockSpec((1, d), lambda b, t, j: (0, 0)),
                   pl.BlockSpec((1, d), lambda b, t, j: (0, 0)),
                   pl.BlockSpec((d, tn), lambda b, t, j: (0, j))]
                  + [pl.BlockSpec((_PERM_ROWS, _PERM_ROWS), lambda b, t, j: (0, 0))] * len(perms)),
        out_specs=([q_spec(g, dl) for g, dl in enumerate(dils)]
                   + [pl.BlockSpec((None, dl, tm // dl, tn),
                                   lambda b, t, j: (b, 0, t, jnp.clip(j - n_groups * nj, 0, nkv - 1)))
                      for dl in dils]),
        out_shape=([jax.ShapeDtypeStruct((bsz, dl, seq // dl, qw), _BF16) for dl in dils]
                   + [jax.ShapeDtypeStruct((bsz, dl, seq // dl, kvw), _BF16) for dl in dils]),
        scratch_shapes=[pltpu.VMEM((n_groups + 1, tm, d), _BF16)],
        compiler_params=_params(("parallel", "parallel", "arbitrary")),
        name="qkv_project",
    )(x2, q_gain.reshape(1, d).astype(_F32), kv_gain.reshape(1, d).astype(_F32), w_cat, *perms)
    return outs[:n_groups], outs[n_groups:]


def _attn_kernel(q_ref, kp_ref, kc_ref, vp_ref, vc_ref, o_ref, st_ref, *, blk, rep, scale):
    i = pl.program_id(2)
    nq = q_ref.shape[0] // blk
    qi = lax.broadcasted_iota(jnp.int32, (blk, 2 * blk), 0)
    si = lax.broadcasted_iota(jnp.int32, (blk, 2 * blk), 1)
    dist = qi + blk - si
    band = (dist >= 0) & (dist <= blk)
    band_first = band & ((i > 0) | (si >= blk))
    lane = lax.broadcasted_iota(jnp.int32, (blk, _LANES), 1)
    n_kv = kc_ref.shape[1] // _HEAD_DIM
    for sub in range(nq):
        valid = jnp.concatenate([band_first if sub == 0 else band] * rep, axis=0)
        rows = slice(sub * blk, (sub + 1) * blk)
        st = jnp.zeros((blk, _LANES), _F32)
        for kvh in range(n_kv):
            ks = slice(kvh * _HEAD_DIM, (kvh + 1) * _HEAD_DIM)
            if sub == 0:
                k = jnp.concatenate([kp_ref[:, ks], kc_ref[0:blk, ks]], axis=0)
                v = jnp.concatenate([vp_ref[:, ks], vc_ref[0:blk, ks]], axis=0)
            else:
                k = kc_ref[(sub - 1) * blk:(sub + 1) * blk, ks]
                v = vc_ref[(sub - 1) * blk:(sub + 1) * blk, ks]
            q = jnp.concatenate(
                [q_ref[rows, (kvh * rep + rp) * _HEAD_DIM:(kvh * rep + rp + 1) * _HEAD_DIM] for rp in range(rep)],
                axis=0)
            s = lax.dot_general(q, k, (((1,), (1,)), ((), ())), preferred_element_type=_F32) * scale
            s = jnp.where(valid, s, _NEG_INF)
            m = jnp.max(s, axis=-1, keepdims=True)
            p = jnp.exp(s - m)
            l = jnp.sum(p, axis=-1, keepdims=True)
            pn = (p * (1.0 / l)).astype(_BF16)
            o = jnp.dot(pn, v, preferred_element_type=_F32)
            lse = m + jnp.log(l)
            for rp in range(rep):
                hd = kvh * rep + rp
                o_ref[rows, hd * _HEAD_DIM:(hd + 1) * _HEAD_DIM] = o[rp * blk:(rp + 1) * blk].astype(o_ref.dtype)
                st = jnp.where(lane == hd, lse[rp * blk:(rp + 1) * blk], st)
        st_ref[rows, :] = st


def _dilated_attention(q, kv, window, dilation, *, nq=2):
    bsz, dil, n, qw = q.shape
    assert dil == dilation
    kw = kv.shape[3] // 2
    n_heads = qw // _HEAD_DIM
    rep = n_heads // (kw // _HEAD_DIM)
    blk = window // dilation
    assert n % (nq * blk) == 0 and blk == _LANES and n_heads <= _LANES
    nb = n // (nq * blk)
    prev = lambda i: jnp.maximum(nq * i - 1, 0)
    return pl.pallas_call(
        functools.partial(_attn_kernel, blk=blk, rep=rep, scale=_HEAD_DIM ** -0.5),
        grid=(bsz, dilation, nb),
        in_specs=[
            pl.BlockSpec((None, None, nq * blk, qw), lambda b, r, i: (b, r, i, 0)),
            pl.BlockSpec((None, None, blk, kw), lambda b, r, i: (b, r, prev(i), 0)),
            pl.BlockSpec((None, None, nq * blk, kw), lambda b, r, i: (b, r, i, 0)),
            pl.BlockSpec((None, None, blk, kw), lambda b, r, i: (b, r, prev(i), 1)),
            pl.BlockSpec((None, None, nq * blk, kw), lambda b, r, i: (b, r, i, 1)),
        ],
        out_specs=[
            pl.BlockSpec((None, None, nq * blk, qw), lambda b, r, i: (b, r, i, 0)),
            pl.BlockSpec((None, None, nq * blk, _LANES), lambda b, r, i: (b, r, i, 0)),
        ],
        out_shape=[
            jax.ShapeDtypeStruct((bsz, dilation, n, qw), _BF16),
            jax.ShapeDtypeStruct((bsz, dilation, n, _LANES), _F32),
        ],
        compiler_params=_params(("parallel", "parallel", "arbitrary")),
        name=f"dilated_attention_d{dilation}",
    )(q, kv, kv, kv, kv)


def _combine_kernel(*refs, dils, n_heads):
    n_groups = len(dils)
    n_p = sum(1 for dl in dils if dl > 1)
    o_in = refs[:n_groups]
    s_in = refs[n_groups:2 * n_groups]
    x_ref, wo_ref = refs[2 * n_groups:2 * n_groups + 2]
    pt_refs = refs[2 * n_groups + 2:2 * n_groups + 2 + n_p]
    out_ref, onat_ref, snat_ref, comb_ref = refs[2 * n_groups + 2 + n_p:]
    tm = x_ref.shape[0]

    p_of = {}
    for dl in dils:
        if dl > 1:
            p_of[dl] = pt_refs[len(p_of)]

    for j in range(tm // _PERM_ROWS):
        rs = slice(j * _PERM_ROWS, (j + 1) * _PERM_ROWS)
        for g, dl in enumerate(dils):
            if dl == 1:
                onat_ref[g, rs, :] = o_in[g][0, rs, :]
                snat_ref[g, rs, :] = s_in[g][0, rs, :]
                continue
            run = _PERM_ROWS // dl
            pt = p_of[dl][...]
            oc = jnp.concatenate([o_in[g][r, j * run:(j + 1) * run, :] for r in range(dl)], axis=0)
            onat_ref[g, rs, :] = jnp.dot(pt, oc, preferred_element_type=_F32).astype(_BF16)
            sc = jnp.concatenate([s_in[g][r, j * run:(j + 1) * run, :] for r in range(dl)], axis=0)
            s_hi = sc.astype(_BF16)
            rem = sc - s_hi.astype(_F32)
            s_mid = rem.astype(_BF16)
            s_lo = (rem - s_mid.astype(_F32)).astype(_BF16)
            snat_ref[g, rs, :] = (jnp.dot(pt, s_hi, preferred_element_type=_F32)
                                  + jnp.dot(pt, s_mid, preferred_element_type=_F32)
                                  + jnp.dot(pt, s_lo, preferred_element_type=_F32))

        lses = [snat_ref[g, rs, :] for g in range(n_groups)]
        mx = lses[0]
        for l in lses[1:]:
            mx = jnp.maximum(mx, l)
        es = [jnp.exp(l - mx) for l in lses]
        den = es[0]
        for e in es[1:]:
            den = den + e
        inv = 1.0 / den
        wts = [e * inv for e in es]
        for hd in range(n_heads):
            cs = slice(hd * _HEAD_DIM, (hd + 1) * _HEAD_DIM)
            base = onat_ref[0, rs, cs].astype(_F32)
            acc = base
            for g in range(1, n_groups):
                acc = acc + wts[g][:, hd:hd + 1] * (onat_ref[g, rs, cs].astype(_F32) - base)
            comb_ref[rs, cs] = acc.astype(_BF16)
        out_ref[rs, :] = x_ref[rs, :] + jnp.dot(comb_ref[rs, :], wo_ref[...], preferred_element_type=_F32)


def _combine_project(outs, stats, x2, w_o, layer, dils, *, tm=512):
    t_tok, d = x2.shape
    bsz, _, _, qw = outs[0].shape
    seq = t_tok // bsz
    nt = seq // tm
    n_groups = len(outs)
    perms_t = [_perm_matrix(dl).T for dl in dils if dl > 1]
    grp = lambda b, t: (b, 0, t, 0)
    return pl.pallas_call(
        functools.partial(_combine_kernel, dils=tuple(dils), n_heads=qw // _HEAD_DIM),
        grid=(bsz, nt),
        in_specs=([pl.BlockSpec((None, dl, tm // dl, qw), grp) for dl in dils]
                  + [pl.BlockSpec((None, dl, tm // dl, _LANES), grp) for dl in dils]
                  + [pl.BlockSpec((tm, d), lambda b, t: (b * nt + t, 0)),
                     pl.BlockSpec((None, qw, d), lambda b, t: (layer, 0, 0))]
                  + [pl.BlockSpec((_PERM_ROWS, _PERM_ROWS), lambda b, t: (0, 0))] * len(perms_t)),
        out_specs=pl.BlockSpec((tm, d), lambda b, t: (b * nt + t, 0)),
        out_shape=jax.ShapeDtypeStruct((t_tok, d), _F32),
        scratch_shapes=[pltpu.VMEM((n_groups, tm, qw), _BF16),
                        pltpu.VMEM((n_groups, tm, _LANES), _F32),
                        pltpu.VMEM((tm, qw), _BF16)],
        compiler_params=_params(("parallel", "arbitrary")),
        name="combine_project",
    )(*outs, *stats, x2, w_o, *perms_t)


def kernel(x, s5_lam_re, s5_lam_im, s5_log_dt, s5_b_re, s5_b_im, s5_c_re, s5_c_im, s5_d, s5_w_glu, a_norm_mix, ffn_norm, ffn_w_in, ffn_w_out, b_norm_mix, attn_w_q, attn_w_o, kv_norm, w_kv, final_norm):
    bsz, seq, d = x.shape
    n_a = a_norm_mix.shape[0]
    n_b = b_norm_mix.shape[0]
    depth = n_a + n_b
    n_groups = len(_DILATED_PATTERNS)
    dils = [dl for _, dl in _DILATED_PATTERNS]
    x2 = x.reshape(bsz * seq, d).astype(_F32)
    w_o_b = attn_w_o.astype(_BF16)
    assert n_b == 1, "the fused q/kv projection covers the single attention layer of this model"
    qw = attn_w_q.shape[2] // n_groups
    for layer in range(depth):
        if layer < n_a:
            i = layer
            lam8, wb, wc = _s5_tables(s5_lam_re[i], s5_lam_im[i], s5_log_dt[i], s5_b_re[i], s5_b_im[i],
                                      s5_c_re[i], s5_c_im[i])
            x2 = _s5_glu_layer(x2, bsz, seq, a_norm_mix[i], s5_d[i], lam8, wb, wc, s5_w_glu[i].astype(_BF16))
        else:
            j = layer - n_a
            w_cat = jnp.concatenate([attn_w_q[j], w_kv], axis=1).astype(_BF16)
            qs, kvs = _qkv_project(x2, bsz, seq, b_norm_mix[j], kv_norm, w_cat, qw, dils)
            outs, stats = [], []
            for g, (window, dilation) in enumerate(_DILATED_PATTERNS):
                o, st = _dilated_attention(qs[g], kvs[g], window, dilation)
                outs.append(o)
                stats.append(st)
            x2 = _combine_project(outs, stats, x2, w_o_b, j, dils)
        final_gain = final_norm if layer == depth - 1 else None
        x2 = _ffn_residual(x2, ffn_norm[layer], ffn_w_in, ffn_w_out, layer, final_gain)
    return x2.reshape(bsz, seq, d).astype(x.dtype)
```

```python
import functools
import math

import jax
import jax.numpy as jnp
from jax import lax
from jax.experimental import pallas as pl
from jax.experimental.pallas import tpu as pltpu

_F32 = jnp.float32
_BF16 = jnp.bfloat16
_EPS = 1e-6
_NEG_INF = -1e30

_LANES = 128
_SUBLANES = 8
_VMEM_LIMIT_BYTES = 60 * 1024 * 1024

_S5_GROUP_CH = 16
_S5_STATE = 64
_HEAD_DIM = 128
_N_KV_HEADS = 4
_DILATED_PATTERNS = ((128, 1), (512, 4), (2048, 16))
_GROUPS_PER_TILE = _LANES // _S5_GROUP_CH
_HALF = _GROUPS_PER_TILE * _S5_STATE
_STATE_W = 2 * _HALF


def _params(sem):
    return pltpu.CompilerParams(dimension_semantics=sem, vmem_limit_bytes=_VMEM_LIMIT_BYTES)


def _rms_scale(x):
    return lax.rsqrt(jnp.mean(x * x, axis=-1, keepdims=True) + _EPS)


def _sigmoid(v):
    return 1.0 / (1.0 + jnp.exp(-v))


def _cmul(ar, ai, br, bi):
    return ar * br - ai * bi, ar * bi + ai * br


def _s5_kernel(x_ref, g_ref, dsk_ref, lam_ref, wb_ref, wc_ref, p_ref, pt_ref, wglu_ref, o_ref,
               h_ref, hbp_ref, y_ref, bu0_ref, bu1_ref, xb0_ref, xb1_ref, carry_ref, cp_ref, lvl_ref,
               zprev_ref, xprev_ref, *, lc, tiles_per_seq):
    s = pl.program_id(0)
    n_tiles = lam_ref.shape[0]
    d_model = o_ref.shape[1]
    n_glu = n_tiles // 2
    glu_w = d_model // n_glu
    rows = lax.broadcasted_iota(jnp.int32, (_SUBLANES, _HALF), 0)

    @pl.when(s % tiles_per_seq == 0)
    def _new_sequence():
        carry_ref[...] = jnp.zeros_like(carry_ref)

    @pl.when(s == 0)
    def _init():
        zprev_ref[...] = jnp.zeros_like(zprev_ref)
        xprev_ref[...] = jnp.zeros_like(xprev_ref)

        def init_m(m, c):
            pr = lam_ref[m, :, 0:_HALF]
            pi = lam_ref[m, :, _HALF:_STATE_W]
            for _ in range(int(math.log2(lc))):
                pr, pi = _cmul(pr, pi, pr, pi)
            qr, qi = pr, pi
            for lv in range(3):
                lvl_ref[m, lv * _SUBLANES:(lv + 1) * _SUBLANES, 0:_HALF] = qr
                lvl_ref[m, lv * _SUBLANES:(lv + 1) * _SUBLANES, _HALF:_STATE_W] = qi
                qr, qi = _cmul(qr, qi, qr, qi)
            cr, ci = pr, pi
            outr, outi = pr, pi
            for c_idx in range(1, _SUBLANES):
                cr, ci = _cmul(cr, ci, pr, pi)
                outr = jnp.where(rows == c_idx, cr, outr)
                outi = jnp.where(rows == c_idx, ci, outi)
            cp_ref[m, :, 0:_HALF] = outr
            cp_ref[m, :, _HALF:_STATE_W] = outi
            return c

        lax.fori_loop(0, n_tiles, init_m, 0)

    x = x_ref[...]
    h = x * _rms_scale(x) * g_ref[...]
    h_ref[...] = h
    hp = jnp.dot(p_ref[...], h.astype(_BF16), preferred_element_type=_F32)
    for m in range(n_tiles):
        hbp_ref[m] = hp[:, m * _LANES:(m + 1) * _LANES].astype(_BF16)

    def project_in(m, bu_ref):
        bu_ref[...] = jnp.dot(hbp_ref[m], wb_ref[m], preferred_element_type=_F32)

    def project_out(m, xb_ref):
        y_ref[m] = jnp.dot(xb_ref[...], wc_ref[m], preferred_element_type=_F32)

    def scan(m, bu_ref, xb_ref):
        lr = lam_ref[m, :, 0:_HALF]
        li = lam_ref[m, :, _HALF:_STATE_W]

        xr = jnp.zeros((_SUBLANES, _HALF), _F32)
        xi = jnp.zeros((_SUBLANES, _HALF), _F32)
        for tau in range(lc):
            r0 = tau * _SUBLANES
            tr, ti = _cmul(lr, li, xr, xi)
            xr = tr + bu_ref[r0:r0 + _SUBLANES, 0:_HALF]
            xi = ti + bu_ref[r0:r0 + _SUBLANES, _HALF:_STATE_W]
            bu_ref[r0:r0 + _SUBLANES, 0:_HALF] = xr
            bu_ref[r0:r0 + _SUBLANES, _HALF:_STATE_W] = xi

        zr, zi = xr, xi
        for lv, s in enumerate((1, 2, 4)):
            ar = lvl_ref[m, lv * _SUBLANES:(lv + 1) * _SUBLANES, 0:_HALF]
            ai = lvl_ref[m, lv * _SUBLANES:(lv + 1) * _SUBLANES, _HALF:_STATE_W]
            sr = jnp.where(rows >= s, pltpu.roll(zr, s, 0), 0.0)
            si = jnp.where(rows >= s, pltpu.roll(zi, s, 0), 0.0)
            tr, ti = _cmul(ar, ai, sr, si)
            zr = zr + tr
            zi = zi + ti
        pr = carry_ref[m, :, 0:_HALF]
        pi = carry_ref[m, :, _HALF:_STATE_W]
        tr, ti = _cmul(cp_ref[m, :, 0:_HALF], cp_ref[m, :, _HALF:_STATE_W], pr, pi)
        er = zr + tr
        ei = zi + ti
        cr = jnp.where(rows >= 1, pltpu.roll(er, 1, 0), pr)
        ci = jnp.where(rows >= 1, pltpu.roll(ei, 1, 0), pi)
        carry_ref[m, :, 0:_HALF] = jnp.broadcast_to(er[_SUBLANES - 1:_SUBLANES, :], (_SUBLANES, _HALF))
        carry_ref[m, :, _HALF:_STATE_W] = jnp.broadcast_to(ei[_SUBLANES - 1:_SUBLANES, :], (_SUBLANES, _HALF))

        for tau in range(0, lc, 2):
            r0 = tau * _SUBLANES
            cr, ci = _cmul(lr, li, cr, ci)
            x0r = bu_ref[r0:r0 + _SUBLANES, 0:_HALF] + cr
            x0i = bu_ref[r0:r0 + _SUBLANES, _HALF:_STATE_W] + ci
            cr, ci = _cmul(lr, li, cr, ci)
            x1r = bu_ref[r0 + _SUBLANES:r0 + 2 * _SUBLANES, 0:_HALF] + cr
            x1i = bu_ref[r0 + _SUBLANES:r0 + 2 * _SUBLANES, _HALF:_STATE_W] + ci
            xb_ref[r0:r0 + 2 * _SUBLANES, 0:_HALF] = jnp.concatenate([x0r, x1r], axis=0).astype(_BF16)
            xb_ref[r0:r0 + 2 * _SUBLANES, _HALF:_STATE_W] = jnp.concatenate([x0i, x1i], axis=0).astype(_BF16)

    def glu_value(k):
        cs = slice(k * glu_w, (k + 1) * glu_w)
        o_ref[:, cs] = jnp.dot(zprev_ref[...], wglu_ref[:, cs], preferred_element_type=_F32)

    def glu_gate(k):
        cs = slice(k * glu_w, (k + 1) * glu_w)
        gate = jnp.dot(zprev_ref[...], wglu_ref[:, d_model + k * glu_w:d_model + (k + 1) * glu_w],
                       preferred_element_type=_F32)
        o_ref[:, cs] = xprev_ref[:, cs] + o_ref[:, cs] * _sigmoid(gate)

    project_in(0, bu0_ref)
    for k in range(n_glu):
        m0 = 2 * k
        project_in(m0 + 1, bu1_ref)
        if k > 0:
            project_out(m0 - 1, xb1_ref)
        glu_value(k)
        scan(m0, bu0_ref, xb0_ref)
        if m0 + 2 < n_tiles:
            project_in(m0 + 2, bu0_ref)
        project_out(m0, xb0_ref)
        glu_gate(k)
        scan(m0 + 1, bu1_ref, xb1_ref)
    project_out(n_tiles - 1, xb1_ref)

    y = jnp.concatenate([y_ref[m] for m in range(n_tiles)], axis=1)
    y_hi = y.astype(_BF16)
    y_lo = (y - y_hi.astype(_F32)).astype(_BF16)
    pt = pt_ref[...]
    yn = (jnp.dot(pt, y_hi, preferred_element_type=_F32)
          + jnp.dot(pt, y_lo, preferred_element_type=_F32))
    yn = yn + dsk_ref[...] * h_ref[...]
    cdf = 0.5 * (1.0 + jnp.tanh(math.sqrt(2.0 / math.pi) * (yn + 0.044715 * (yn * yn * yn))))
    zprev_ref[...] = (yn * cdf).astype(zprev_ref.dtype)
    xprev_ref[...] = x_ref[...]


def _s5_tables(lam_re, lam_im, log_dt, b_re, b_im, c_re, c_im):
    g, p = lam_re.shape
    c = b_re.shape[-1]
    nm = g // _GROUPS_PER_TILE
    lr = lam_re.astype(_F32)
    li = lam_im.astype(_F32)
    dt = jnp.exp(log_dt.astype(_F32))[:, None]
    mag = jnp.exp(lr * dt)
    ang = li * dt
    lb_re = mag * jnp.cos(ang)
    lb_im = mag * jnp.sin(ang)
    nr = lb_re - 1.0
    den = lr * lr + li * li
    f_re = (nr * lr + lb_im * li) / den
    f_im = (lb_im * lr - nr * li) / den
    br = b_re.astype(_F32)
    bi = b_im.astype(_F32)
    bb_re = f_re[..., None] * br - f_im[..., None] * bi
    bb_im = f_re[..., None] * bi + f_im[..., None] * br
    eye = jnp.eye(_GROUPS_PER_TILE, dtype=_F32)

    bb = jnp.stack([bb_re, bb_im]).reshape(2, nm, _GROUPS_PER_TILE, p, c)
    wb = jnp.einsum('rmhpc,gh->mgcrhp', bb, eye)
    wb = wb.reshape(nm, _GROUPS_PER_TILE * c, 2 * _GROUPS_PER_TILE * p).astype(_BF16)
    cc = jnp.stack([c_re.astype(_F32), -c_im.astype(_F32)]).reshape(2, nm, _GROUPS_PER_TILE, c, p)
    wc = jnp.einsum('rmgcp,gh->mrgphc', cc, eye)
    wc = wc.reshape(nm, 2 * _GROUPS_PER_TILE * p, _GROUPS_PER_TILE * c).astype(_BF16)
    lam = jnp.concatenate([lb_re.reshape(nm, _HALF), lb_im.reshape(nm, _HALF)], axis=-1)
    lam8 = jnp.broadcast_to(lam[:, None, :], (nm, _SUBLANES, _STATE_W))
    return lam8, wb, wc


def _s5_glu_layer(x2, bsz, seq, gain, d_skip, lam8, wb, wc, w_glu, *, tile=256):
    t_tok, d = x2.shape
    nm = lam8.shape[0]
    lc = tile // _SUBLANES
    nt = seq // tile
    n_steps = bsz * nt
    assert (d // (nm // 2)) % _LANES == 0
    r = jnp.arange(tile)
    col = (r % _SUBLANES) * lc + r // _SUBLANES
    perm = (col[:, None] == jnp.arange(tile)[None, :]).astype(_BF16)
    perm_t = perm.T
    const3 = lambda s: (0, 0, 0)
    const2 = lambda s: (0, 0)
    once = dict(pipeline_mode=pl.Buffered(1))
    return pl.pallas_call(
        functools.partial(_s5_kernel, lc=lc, tiles_per_seq=nt),
        grid=(n_steps + 1,),
        in_specs=[
            pl.BlockSpec((tile, d), lambda s: (jnp.minimum(s, n_steps - 1), 0)),
            pl.BlockSpec((1, d), const2),
            pl.BlockSpec((1, d), const2),
            pl.BlockSpec((nm, _SUBLANES, _STATE_W), const3, **once),
            pl.BlockSpec((nm, _LANES, _STATE_W), const3, **once),
            pl.BlockSpec((nm, _STATE_W, _LANES), const3, **once),
            pl.BlockSpec((tile, tile), const2),
            pl.BlockSpec((tile, tile), const2),
            pl.BlockSpec((d, 2 * d), const2, **once),
        ],
        out_specs=pl.BlockSpec((tile, d), lambda s: (jnp.maximum(s - 1, 0), 0)),
        out_shape=jax.ShapeDtypeStruct((t_tok, d), _F32),
        scratch_shapes=[
            pltpu.VMEM((tile, d), _F32),
            pltpu.VMEM((nm, tile, _LANES), _BF16),
            pltpu.VMEM((nm, tile, _LANES), _F32),
            pltpu.VMEM((tile, _STATE_W), _F32),
            pltpu.VMEM((tile, _STATE_W), _F32),
            pltpu.VMEM((tile, _STATE_W), _BF16),
            pltpu.VMEM((tile, _STATE_W), _BF16),
            pltpu.VMEM((nm, _SUBLANES, _STATE_W), _F32),
            pltpu.VMEM((nm, _SUBLANES, _STATE_W), _F32),
            pltpu.VMEM((nm, 3 * _SUBLANES, _STATE_W), _F32),
            pltpu.VMEM((tile, d), _BF16),
            pltpu.VMEM((tile, d), _F32),
        ],
        compiler_params=_params(("arbitrary",)),
        name="s5_glu_layer",
    )(x2, gain.reshape(1, d).astype(_F32), d_skip.reshape(1, d).astype(_F32), lam8, wb, wc, perm, perm_t, w_glu)


def _ffn_kernel(x_ref, g_ref, fg_ref, win_hbm, wout_hbm, o_ref, hn_ref, wg_buf, wu_buf, wo_buf, sem,
                *, layer, hidden, tf, final_norm, row_split):
    i = pl.program_id(0)
    n_i = pl.num_programs(0)
    nf = hidden // tf
    rows = o_ref.shape[0] // row_split
    slabs = [slice(s * rows, (s + 1) * rows) for s in range(row_split)]

    def copies(f, slot):
        return (
            pltpu.make_async_copy(win_hbm.at[layer, :, pl.ds(f * tf, tf)], wg_buf.at[slot], sem.at[0, slot]),
            pltpu.make_async_copy(win_hbm.at[layer, :, pl.ds(hidden + f * tf, tf)], wu_buf.at[slot],
                                  sem.at[1, slot]),
            pltpu.make_async_copy(wout_hbm.at[layer, pl.ds(f * tf, tf), :], wo_buf.at[slot], sem.at[2, slot]),
        )

    def start(f, slot):
        for cp in copies(f, slot):
            cp.start()

    def wait(f, slot):
        for cp in copies(f, slot):
            cp.wait()

    n_slots = wg_buf.shape[0]
    ahead = n_slots - 1
    total = n_i * nf

    @pl.when(i == 0)
    def _first_fetches():
        for g0 in range(ahead):
            start(g0 % nf, g0 % n_slots)

    for sl in slabs:
        x = x_ref[sl, :]
        hn_ref[sl, :] = (x * _rms_scale(x) * g_ref[...]).astype(_BF16)
        o_ref[sl, :] = x

    def f_body(f, c):
        g = i * nf + f
        slot = lax.rem(g, n_slots)
        wait(f, slot)

        @pl.when(g + ahead < total)
        def _prefetch():
            start(lax.rem(f + ahead, nf), lax.rem(g + ahead, n_slots))

        wg = wg_buf[slot].astype(_BF16)
        wu = wu_buf[slot].astype(_BF16)
        wo = wo_buf[slot].astype(_BF16)
        for sl in slabs:
            hn = hn_ref[sl, :]
            a = jnp.dot(hn, wg, preferred_element_type=_F32)
            u = jnp.dot(hn, wu, preferred_element_type=_F32)
            act = (a * _sigmoid(a) * u).astype(_BF16)
            o_ref[sl, :] += jnp.dot(act, wo, preferred_element_type=_F32)
        return c

    lax.fori_loop(0, nf, f_body, 0)

    if final_norm:
        for sl in slabs:
            y = o_ref[sl, :]
            o_ref[sl, :] = y * _rms_scale(y) * fg_ref[...]


def _ffn_residual(x2, gain, w_in, w_out, layer, final_gain=None, *, tm=1024, tf=256, row_split=2, n_slots=3):
    t_tok, d = x2.shape
    hidden = w_out.shape[1]
    final_norm = final_gain is not None
    fg = (final_gain if final_norm else gain).reshape(1, d).astype(_F32)
    return pl.pallas_call(
        functools.partial(_ffn_kernel, layer=layer, hidden=hidden, tf=tf, final_norm=final_norm,
                          row_split=row_split),
        grid=(t_tok // tm,),
        in_specs=[
            pl.BlockSpec((tm, d), lambda i: (i, 0)),
            pl.BlockSpec((1, d), lambda i: (0, 0)),
            pl.BlockSpec((1, d), lambda i: (0, 0)),
            pl.BlockSpec(memory_space=pl.ANY),
            pl.BlockSpec(memory_space=pl.ANY),
        ],
        out_specs=pl.BlockSpec((tm, d), lambda i: (i, 0)),
        out_shape=jax.ShapeDtypeStruct((t_tok, d), _F32),
        scratch_shapes=[
            pltpu.VMEM((tm, d), _BF16),
            pltpu.VMEM((n_slots, d, tf), w_in.dtype),
            pltpu.VMEM((n_slots, d, tf), w_in.dtype),
            pltpu.VMEM((n_slots, tf, d), w_out.dtype),
            pltpu.SemaphoreType.DMA((3, n_slots)),
        ],
        compiler_params=_params(("arbitrary",)),
        name="ffn_residual",
    )(x2, gain.reshape(1, d).astype(_F32), fg, w_in, w_out)


_PERM_ROWS = 256


def _perm_matrix(dil, rows=_PERM_ROWS):
    dst = jnp.arange(rows)
    src = (dst % (rows // dil)) * dil + dst // (rows // dil)
    return (src[:, None] == jnp.arange(rows)[None, :]).astype(_BF16)


def _to_residue_major(y, p_ref, dil, store):
    tm = y.shape[0]
    run = _PERM_ROWS // dil
    for j in range(tm // _PERM_ROWS):
        yp = jnp.dot(p_ref[...], y[j * _PERM_ROWS:(j + 1) * _PERM_ROWS], preferred_element_type=_F32).astype(_BF16)
        for r in range(dil):
            store(r, j * run, yp[r * run:(r + 1) * run])


def _qkv_kernel(*refs, dils, nj):
    n_groups = len(dils)
    n_p = sum(1 for dl in dils if dl > 1)
    x_ref, gq_ref, gkv_ref, wq_ref, wkv_ref = refs[:5]
    p_refs = refs[5:5 + n_p]
    q_refs = refs[5 + n_p:5 + n_p + n_groups]
    kv_refs = refs[5 + n_p + n_groups:5 + n_p + 2 * n_groups]
    hn_ref = refs[-1]
    tm = hn_ref.shape[1]
    p_of = {}
    for dl in dils:
        if dl > 1:
            p_of[dl] = p_refs[len(p_of)]
    j = pl.program_id(2)

    @pl.when(j == 0)
    def _start():
        for sb in range(tm // _PERM_ROWS):
            rows = slice(sb * _PERM_ROWS, (sb + 1) * _PERM_ROWS)
            x = x_ref[rows, :]
            xr = x * _rms_scale(x)
            hq = (xr * gq_ref[...]).astype(_BF16)
            hn_ref[n_groups, rows, :] = (xr * gkv_ref[...]).astype(_BF16)
            for g, dl in enumerate(dils):
                if dl == 1:
                    hn_ref[g, rows, :] = hq
                else:
                    run = _PERM_ROWS // dl
                    yp = jnp.dot(p_of[dl][...], hq, preferred_element_type=_F32).astype(_BF16)
                    for r in range(dl):
                        dst = r * (tm // dl) + sb * run
                        hn_ref[g, dst:dst + run, :] = yp[r * run:(r + 1) * run]

    for g, dl in enumerate(dils):
        @pl.when((j >= g * nj) & (j < (g + 1) * nj))
        def _q_tile(g=g, dl=dl):
            res = jnp.dot(hn_ref[g], wq_ref[...].astype(_BF16), preferred_element_type=_F32).astype(_BF16)
            for r in range(dl):
                q_refs[g][r] = res[r * (tm // dl):(r + 1) * (tm // dl)]

    @pl.when(j >= n_groups * nj)
    def _kv_tile():
        kv = jnp.dot(hn_ref[n_groups], wkv_ref[...], preferred_element_type=_F32).astype(_BF16)
        for g, dl in enumerate(dils):
            if dl == 1:
                kv_refs[g][0] = kv
            else:
                def store(r, row, block, g=g):
                    kv_refs[g][r, row:row + block.shape[0], :] = block
                _to_residue_major(kv, p_of[dl], dl, store)


def _qkv_project(x2, bsz, seq, q_gain, kv_gain, w_q, layer, w_kv, dils, *, tm=1024, tn=512):
    t_tok, d = x2.shape
    n_groups = len(dils)
    qw = w_q.shape[2] // n_groups
    kvw = w_kv.shape[1]
    assert qw % tn == 0 and kvw % tn == 0
    nj = qw // tn
    nkv = kvw // tn
    nt = seq // tm
    perms = [_perm_matrix(dl) for dl in dils if dl > 1]

    def q_spec(g, dl):
        return pl.BlockSpec((None, dl, tm // dl, tn),
                            lambda b, t, j: (b, 0, t, jnp.clip(j - g * nj, 0, nj - 1)))

    outs = pl.pallas_call(
        functools.partial(_qkv_kernel, dils=tuple(dils), nj=nj),
        grid=(bsz, nt, n_groups * nj + nkv),
        in_specs=([pl.BlockSpec((tm, d), lambda b, t, j: (b * nt + t, 0)),
                   pl.BlockSpec((1, d), lambda b, t, j: (0, 0)),
                   pl.BlockSpec((1, d), lambda b, t, j: (0, 0)),
                   pl.BlockSpec((None, d, tn), lambda b, t, j: (layer, 0, jnp.minimum(j, n_groups * nj - 1))),
                   pl.BlockSpec((d, tn), lambda b, t, j: (0, jnp.clip(j - n_groups * nj, 0, nkv - 1)))]
                  + [pl.BlockSpec((_PERM_ROWS, _PERM_ROWS), lambda b, t, j: (0, 0))] * len(perms)),
        out_specs=([q_spec(g, dl) for g, dl in enumerate(dils)]
                   + [pl.BlockSpec((None, dl, tm // dl, tn),
                                   lambda b, t, j: (b, 0, t, jnp.clip(j - n_groups * nj, 0, nkv - 1)))
                      for dl in dils]),
        out_shape=([jax.ShapeDtypeStruct((bsz, dl, seq // dl, qw), _BF16) for dl in dils]
                   + [jax.ShapeDtypeStruct((bsz, dl, seq // dl, kvw), _BF16) for dl in dils]),
        scratch_shapes=[pltpu.VMEM((n_groups + 1, tm, d), _BF16)],
        compiler_params=_params(("parallel", "parallel", "arbitrary")),
        name="qkv_project",
    )(x2, q_gain.reshape(1, d).astype(_F32), kv_gain.reshape(1, d).astype(_F32), w_q, w_kv, *perms)
    return outs[:n_groups], outs[n_groups:]


def _attn_kernel(q_ref, kp_ref, kc_ref, vp_ref, vc_ref, o_ref, st_ref, *, blk, rep, scale):
    i = pl.program_id(2)
    nq = q_ref.shape[0] // blk
    qi = lax.broadcasted_iota(jnp.int32, (blk, 2 * blk), 0)
    si = lax.broadcasted_iota(jnp.int32, (blk, 2 * blk), 1)
    dist = qi + blk - si
    band = (dist >= 0) & (dist <= blk)
    band_first = band & ((i > 0) | (si >= blk))
    lane = lax.broadcasted_iota(jnp.int32, (blk, _LANES), 1)
    n_kv = kc_ref.shape[1] // _HEAD_DIM
    for sub in range(nq):
        valid = jnp.concatenate([band_first if sub == 0 else band] * rep, axis=0)
        rows = slice(sub * blk, (sub + 1) * blk)
        st = jnp.zeros((blk, _LANES), _F32)
        for kvh in range(n_kv):
            ks = slice(kvh * _HEAD_DIM, (kvh + 1) * _HEAD_DIM)
            if sub == 0:
                k = jnp.concatenate([kp_ref[:, ks], kc_ref[0:blk, ks]], axis=0)
                v = jnp.concatenate([vp_ref[:, ks], vc_ref[0:blk, ks]], axis=0)
            else:
                k = kc_ref[(sub - 1) * blk:(sub + 1) * blk, ks]
                v = vc_ref[(sub - 1) * blk:(sub + 1) * blk, ks]
            q = jnp.concatenate(
                [q_ref[rows, (kvh * rep + rp) * _HEAD_DIM:(kvh * rep + rp + 1) * _HEAD_DIM] for rp in range(rep)],
                axis=0)
            s = lax.dot_general(q, k, (((1,), (1,)), ((), ())), preferred_element_type=_F32) * scale
            s = jnp.where(valid, s, _NEG_INF)
            m = jnp.max(s, axis=-1, keepdims=True)
            p = jnp.exp(s - m)
            l = jnp.sum(p, axis=-1, keepdims=True)
            pn = (p * (1.0 / l)).astype(_BF16)
            o = jnp.dot(pn, v, preferred_element_type=_F32)
            lse = m + jnp.log(l)
            for rp in range(rep):
                hd = kvh * rep + rp
                o_ref[rows, hd * _HEAD_DIM:(hd + 1) * _HEAD_DIM] = o[rp * blk:(rp + 1) * blk].astype(o_ref.dtype)
                st = jnp.where(lane == hd, lse[rp * blk:(rp + 1) * blk], st)
        st_ref[rows, :] = st


def _dilated_attention(q, kv, window, dilation, *, nq=2):
    bsz, dil, n, qw = q.shape
    assert dil == dilation
    kw = kv.shape[3] // 2
    n_heads = qw // _HEAD_DIM
    rep = n_heads // (kw // _HEAD_DIM)
    blk = window // dilation
    assert n % (nq * blk) == 0 and blk == _LANES and n_heads <= _LANES
    nb = n // (nq * blk)
    prev = lambda i: jnp.maximum(nq * i - 1, 0)
    return pl.pallas_call(
        functools.partial(_attn_kernel, blk=blk, rep=rep, scale=_HEAD_DIM ** -0.5),
        grid=(bsz, dilation, nb),
        in_specs=[
            pl.BlockSpec((None, None, nq * blk, qw), lambda b, r, i: (b, r, i, 0)),
            pl.BlockSpec((None, None, blk, kw), lambda b, r, i: (b, r, prev(i), 0)),
            pl.BlockSpec((None, None, nq * blk, kw), lambda b, r, i: (b, r, i, 0)),
            pl.BlockSpec((None, None, blk, kw), lambda b, r, i: (b, r, prev(i), 1)),
            pl.BlockSpec((None, None, nq * blk, kw), lambda b, r, i: (b, r, i, 1)),
        ],
        out_specs=[
            pl.BlockSpec((None, None, nq * blk, qw), lambda b, r, i: (b, r, i, 0)),
            pl.BlockSpec((None, None, nq * blk, _LANES), lambda b, r, i: (b, r, i, 0)),
        ],
        out_shape=[
            jax.ShapeDtypeStruct((bsz, dilation, n, qw), _BF16),
            jax.ShapeDtypeStruct((bsz, dilation, n, _LANES), _F32),
        ],
        compiler_params=_params(("parallel", "parallel", "arbitrary")),
        name=f"dilated_attention_d{dilation}",
    )(q, kv, kv, kv, kv)


def _combine_kernel(*refs, dils, n_heads):
    n_groups = len(dils)
    n_p = sum(1 for dl in dils if dl > 1)
    o_in = refs[:n_groups]
    s_in = refs[n_groups:2 * n_groups]
    x_ref, wo_ref = refs[2 * n_groups:2 * n_groups + 2]
    pt_refs = refs[2 * n_groups + 2:2 * n_groups + 2 + n_p]
    out_ref, onat_ref, snat_ref, comb_ref = refs[2 * n_groups + 2 + n_p:]
    tm = x_ref.shape[0]

    p_of = {}
    for dl in dils:
        if dl > 1:
            p_of[dl] = pt_refs[len(p_of)]

    for j in range(tm // _PERM_ROWS):
        rs = slice(j * _PERM_ROWS, (j + 1) * _PERM_ROWS)
        for g, dl in enumerate(dils):
            if dl == 1:
                onat_ref[g, rs, :] = o_in[g][0, rs, :]
                snat_ref[g, rs, :] = s_in[g][0, rs, :]
                continue
            run = _PERM_ROWS // dl
            pt = p_of[dl][...]
            oc = jnp.concatenate([o_in[g][r, j * run:(j + 1) * run, :] for r in range(dl)], axis=0)
            onat_ref[g, rs, :] = jnp.dot(pt, oc, preferred_element_type=_F32).astype(_BF16)
            sc = jnp.concatenate([s_in[g][r, j * run:(j + 1) * run, :] for r in range(dl)], axis=0)
            s_hi = sc.astype(_BF16)
            rem = sc - s_hi.astype(_F32)
            s_mid = rem.astype(_BF16)
            s_lo = (rem - s_mid.astype(_F32)).astype(_BF16)
            snat_ref[g, rs, :] = (jnp.dot(pt, s_hi, preferred_element_type=_F32)
                                  + jnp.dot(pt, s_mid, preferred_element_type=_F32)
                                  + jnp.dot(pt, s_lo, preferred_element_type=_F32))

        lses = [snat_ref[g, rs, :] for g in range(n_groups)]
        mx = lses[0]
        for l in lses[1:]:
            mx = jnp.maximum(mx, l)
        es = [jnp.exp(l - mx) for l in lses]
        den = es[0]
        for e in es[1:]:
            den = den + e
        inv = 1.0 / den
        wts = [e * inv for e in es]
        for hd in range(n_heads):
            cs = slice(hd * _HEAD_DIM, (hd + 1) * _HEAD_DIM)
            base = onat_ref[0, rs, cs].astype(_F32)
            acc = base
            for g in range(1, n_groups):
                acc = acc + wts[g][:, hd:hd + 1] * (onat_ref[g, rs, cs].astype(_F32) - base)
            comb_ref[rs, cs] = acc.astype(_BF16)
        out_ref[rs, :] = x_ref[rs, :] + jnp.dot(comb_ref[rs, :], wo_ref[...], preferred_element_type=_F32)


def _combine_project(outs, stats, x2, w_o, layer, dils, *, tm=512):
    t_tok, d = x2.shape
    bsz, _, _, qw = outs[0].shape
    seq = t_tok // bsz
    nt = seq // tm
    n_groups = len(outs)
    perms_t = [_perm_matrix(dl).T for dl in dils if dl > 1]
    grp = lambda b, t: (b, 0, t, 0)
    return pl.pallas_call(
        functools.partial(_combine_kernel, dils=tuple(dils), n_heads=qw // _HEAD_DIM),
        grid=(bsz, nt),
        in_specs=([pl.BlockSpec((None, dl, tm // dl, qw), grp) for dl in dils]
                  + [pl.BlockSpec((None, dl, tm // dl, _LANES), grp) for dl in dils]
                  + [pl.BlockSpec((tm, d), lambda b, t: (b * nt + t, 0)),
                     pl.BlockSpec((None, qw, d), lambda b, t: (layer, 0, 0), pipeline_mode=pl.Buffered(1))]
                  + [pl.BlockSpec((_PERM_ROWS, _PERM_ROWS), lambda b, t: (0, 0))] * len(perms_t)),
        out_specs=pl.BlockSpec((tm, d), lambda b, t: (b * nt + t, 0)),
        out_shape=jax.ShapeDtypeStruct((t_tok, d), _F32),
        scratch_shapes=[pltpu.VMEM((n_groups, tm, qw), _BF16),
                        pltpu.VMEM((n_groups, tm, _LANES), _F32),
                        pltpu.VMEM((tm, qw), _BF16)],
        compiler_params=_params(("parallel", "arbitrary")),
        name="combine_project",
    )(*outs, *stats, x2, w_o, *perms_t)


def kernel(x, s5_lam_re, s5_lam_im, s5_log_dt, s5_b_re, s5_b_im, s5_c_re, s5_c_im, s5_d, s5_w_glu, a_norm_mix, ffn_norm, ffn_w_in, ffn_w_out, b_norm_mix, attn_w_q, attn_w_o, kv_norm, w_kv, final_norm):
    bsz, seq, d = x.shape
    n_a = a_norm_mix.shape[0]
    n_b = b_norm_mix.shape[0]
    depth = n_a + n_b
    n_groups = len(_DILATED_PATTERNS)
    dils = [dl for _, dl in _DILATED_PATTERNS]
    x2 = x.reshape(bsz * seq, d).astype(_F32)
    assert n_b == 1, "the fused q/kv projection covers the single attention layer of this model"
    for layer in range(depth):
        if layer < n_a:
            i = layer
            lam8, wb, wc = _s5_tables(s5_lam_re[i], s5_lam_im[i], s5_log_dt[i], s5_b_re[i], s5_b_im[i],
                                      s5_c_re[i], s5_c_im[i])
            x2 = _s5_glu_layer(x2, bsz, seq, a_norm_mix[i], s5_d[i], lam8, wb, wc, s5_w_glu[i].astype(_BF16))
        else:
            j = layer - n_a
            qs, kvs = _qkv_project(x2, bsz, seq, b_norm_mix[j], kv_norm, attn_w_q, j, w_kv.astype(_BF16), dils)
            outs, stats = [], []
            for g, (window, dilation) in enumerate(_DILATED_PATTERNS):
                o, st = _dilated_attention(qs[g], kvs[g], window, dilation)
                outs.append(o)
                stats.append(st)
            x2 = _combine_project(outs, stats, x2, attn_w_o.astype(_BF16), j, dils)
        final_gain = final_norm if layer == depth - 1 else None
        x2 = _ffn_residual(x2, ffn_norm[layer], ffn_w_in, ffn_w_out, layer, final_gain)
    return x2.reshape(bsz, seq, d).astype(x.dtype)
```

```python
import functools
import math

import jax
import jax.numpy as jnp
from jax import lax
from jax.experimental import pallas as pl
from jax.experimental.pallas import tpu as pltpu

_F32 = jnp.float32
_BF16 = jnp.bfloat16
_EPS = 1e-6
_NEG_INF = -1e30

_LANES = 128
_SUBLANES = 8
_VMEM_LIMIT_BYTES = 60 * 1024 * 1024

_S5_GROUP_CH = 16
_S5_STATE = 64
_HEAD_DIM = 128
_N_KV_HEADS = 4
_DILATED_PATTERNS = ((128, 1), (512, 4), (2048, 16))
_GROUPS_PER_TILE = _LANES // _S5_GROUP_CH
_HALF = _GROUPS_PER_TILE * _S5_STATE
_STATE_W = 2 * _HALF


def _params(sem):
    return pltpu.CompilerParams(dimension_semantics=sem, vmem_limit_bytes=_VMEM_LIMIT_BYTES)


def _rms_scale(x):
    return lax.rsqrt(jnp.mean(x * x, axis=-1, keepdims=True) + _EPS)


def _sigmoid(v):
    return 1.0 / (1.0 + jnp.exp(-v))


def _cmul(ar, ai, br, bi):
    return ar * br - ai * bi, ar * bi + ai * br


def _s5_kernel(x_ref, g_ref, dsk_ref, lam_ref, wb_ref, wc_ref, p_ref, pt_ref, wglu_ref, o_ref,
               h_ref, hbp_ref, y_ref, bu0_ref, bu1_ref, xb0_ref, xb1_ref, carry_ref, cp_ref, lvl_ref,
               zprev_ref, xprev_ref, *, lc, tiles_per_seq):
    s = pl.program_id(0)
    n_tiles = lam_ref.shape[0]
    d_model = o_ref.shape[1]
    n_glu = n_tiles // 2
    glu_w = d_model // n_glu
    rows = lax.broadcasted_iota(jnp.int32, (_SUBLANES, _HALF), 0)

    @pl.when(s % tiles_per_seq == 0)
    def _new_sequence():
        carry_ref[...] = jnp.zeros_like(carry_ref)

    @pl.when(s == 0)
    def _init():
        zprev_ref[...] = jnp.zeros_like(zprev_ref)
        xprev_ref[...] = jnp.zeros_like(xprev_ref)

        def init_m(m, c):
            pr = lam_ref[m, :, 0:_HALF]
            pi = lam_ref[m, :, _HALF:_STATE_W]
            for _ in range(int(math.log2(lc))):
                pr, pi = _cmul(pr, pi, pr, pi)
            qr, qi = pr, pi
            for lv in range(3):
                lvl_ref[m, lv * _SUBLANES:(lv + 1) * _SUBLANES, 0:_HALF] = qr
                lvl_ref[m, lv * _SUBLANES:(lv + 1) * _SUBLANES, _HALF:_STATE_W] = qi
                qr, qi = _cmul(qr, qi, qr, qi)
            cr, ci = pr, pi
            outr, outi = pr, pi
            for c_idx in range(1, _SUBLANES):
                cr, ci = _cmul(cr, ci, pr, pi)
                outr = jnp.where(rows == c_idx, cr, outr)
                outi = jnp.where(rows == c_idx, ci, outi)
            cp_ref[m, :, 0:_HALF] = outr
            cp_ref[m, :, _HALF:_STATE_W] = outi
            return c

        lax.fori_loop(0, n_tiles, init_m, 0)

    x = x_ref[...]
    h = x * _rms_scale(x) * g_ref[...]
    h_ref[...] = h
    hp = jnp.dot(p_ref[...], h.astype(_BF16), preferred_element_type=_F32)
    for m in range(n_tiles):
        hbp_ref[m] = hp[:, m * _LANES:(m + 1) * _LANES].astype(_BF16)

    def project_in(m, bu_ref):
        bu_ref[...] = jnp.dot(hbp_ref[m], wb_ref[m], preferred_element_type=_F32)

    def project_out(m, xb_ref):
        y_ref[m] = jnp.dot(xb_ref[...], wc_ref[m], preferred_element_type=_F32)

    def scan(m, bu_ref, xb_ref):
        lr = lam_ref[m, :, 0:_HALF]
        li = lam_ref[m, :, _HALF:_STATE_W]

        xr = jnp.zeros((_SUBLANES, _HALF), _F32)
        xi = jnp.zeros((_SUBLANES, _HALF), _F32)
        for tau in range(lc):
            r0 = tau * _SUBLANES
            tr, ti = _cmul(lr, li, xr, xi)
            xr = tr + bu_ref[r0:r0 + _SUBLANES, 0:_HALF]
            xi = ti + bu_ref[r0:r0 + _SUBLANES, _HALF:_STATE_W]
            bu_ref[r0:r0 + _SUBLANES, 0:_HALF] = xr
            bu_ref[r0:r0 + _SUBLANES, _HALF:_STATE_W] = xi

        zr, zi = xr, xi
        for lv, s in enumerate((1, 2, 4)):
            ar = lvl_ref[m, lv * _SUBLANES:(lv + 1) * _SUBLANES, 0:_HALF]
            ai = lvl_ref[m, lv * _SUBLANES:(lv + 1) * _SUBLANES, _HALF:_STATE_W]
            sr = jnp.where(rows >= s, pltpu.roll(zr, s, 0), 0.0)
            si = jnp.where(rows >= s, pltpu.roll(zi, s, 0), 0.0)
            tr, ti = _cmul(ar, ai, sr, si)
            zr = zr + tr
            zi = zi + ti
        pr = carry_ref[m, :, 0:_HALF]
        pi = carry_ref[m, :, _HALF:_STATE_W]
        tr, ti = _cmul(cp_ref[m, :, 0:_HALF], cp_ref[m, :, _HALF:_STATE_W], pr, pi)
        er = zr + tr
        ei = zi + ti
        cr = jnp.where(rows >= 1, pltpu.roll(er, 1, 0), pr)
        ci = jnp.where(rows >= 1, pltpu.roll(ei, 1, 0), pi)
        carry_ref[m, :, 0:_HALF] = jnp.broadcast_to(er[_SUBLANES - 1:_SUBLANES, :], (_SUBLANES, _HALF))
        carry_ref[m, :, _HALF:_STATE_W] = jnp.broadcast_to(ei[_SUBLANES - 1:_SUBLANES, :], (_SUBLANES, _HALF))

        for tau in range(0, lc, 2):
            r0 = tau * _SUBLANES
            cr, ci = _cmul(lr, li, cr, ci)
            x0r = bu_ref[r0:r0 + _SUBLANES, 0:_HALF] + cr
            x0i = bu_ref[r0:r0 + _SUBLANES, _HALF:_STATE_W] + ci
            cr, ci = _cmul(lr, li, cr, ci)
            x1r = bu_ref[r0 + _SUBLANES:r0 + 2 * _SUBLANES, 0:_HALF] + cr
            x1i = bu_ref[r0 + _SUBLANES:r0 + 2 * _SUBLANES, _HALF:_STATE_W] + ci
            xb_ref[r0:r0 + 2 * _SUBLANES, 0:_HALF] = jnp.concatenate([x0r, x1r], axis=0).astype(_BF16)
            xb_ref[r0:r0 + 2 * _SUBLANES, _HALF:_STATE_W] = jnp.concatenate([x0i, x1i], axis=0).astype(_BF16)

    def glu_value(k):
        cs = slice(k * glu_w, (k + 1) * glu_w)
        o_ref[:, cs] = jnp.dot(zprev_ref[...], wglu_ref[:, cs], preferred_element_type=_F32)

    def glu_gate(k):
        cs = slice(k * glu_w, (k + 1) * glu_w)
        gate = jnp.dot(zprev_ref[...], wglu_ref[:, d_model + k * glu_w:d_model + (k + 1) * glu_w],
                       preferred_element_type=_F32)
        o_ref[:, cs] = xprev_ref[:, cs] + o_ref[:, cs] * _sigmoid(gate)

    project_in(0, bu0_ref)
    for k in range(n_glu):
        m0 = 2 * k
        project_in(m0 + 1, bu1_ref)
        if k > 0:
            project_out(m0 - 1, xb1_ref)
        glu_value(k)
        scan(m0, bu0_ref, xb0_ref)
        if m0 + 2 < n_tiles:
            project_in(m0 + 2, bu0_ref)
        project_out(m0, xb0_ref)
        glu_gate(k)
        scan(m0 + 1, bu1_ref, xb1_ref)
    project_out(n_tiles - 1, xb1_ref)

    y = jnp.concatenate([y_ref[m] for m in range(n_tiles)], axis=1)
    y_hi = y.astype(_BF16)
    y_lo = (y - y_hi.astype(_F32)).astype(_BF16)
    pt = pt_ref[...]
    yn = (jnp.dot(pt, y_hi, preferred_element_type=_F32)
          + jnp.dot(pt, y_lo, preferred_element_type=_F32))
    yn = yn + dsk_ref[...] * h_ref[...]
    cdf = 0.5 * (1.0 + jnp.tanh(math.sqrt(2.0 / math.pi) * (yn + 0.044715 * (yn * yn * yn))))
    zprev_ref[...] = (yn * cdf).astype(zprev_ref.dtype)
    xprev_ref[...] = x_ref[...]


def _s5_tables(lam_re, lam_im, log_dt, b_re, b_im, c_re, c_im):
    g, p = lam_re.shape
    c = b_re.shape[-1]
    nm = g // _GROUPS_PER_TILE
    lr = lam_re.astype(_F32)
    li = lam_im.astype(_F32)
    dt = jnp.exp(log_dt.astype(_F32))[:, None]
    mag = jnp.exp(lr * dt)
    ang = li * dt
    lb_re = mag * jnp.cos(ang)
    lb_im = mag * jnp.sin(ang)
    nr = lb_re - 1.0
    den = lr * lr + li * li
    f_re = (nr * lr + lb_im * li) / den
    f_im = (lb_im * lr - nr * li) / den
    br = b_re.astype(_F32)
    bi = b_im.astype(_F32)
    bb_re = f_re[..., None] * br - f_im[..., None] * bi
    bb_im = f_re[..., None] * bi + f_im[..., None] * br
    eye = jnp.eye(_GROUPS_PER_TILE, dtype=_F32)

    bb = jnp.stack([bb_re, bb_im]).reshape(2, nm, _GROUPS_PER_TILE, p, c)
    wb = jnp.einsum('rmhpc,gh->mgcrhp', bb, eye)
    wb = wb.reshape(nm, _GROUPS_PER_TILE * c, 2 * _GROUPS_PER_TILE * p).astype(_BF16)
    cc = jnp.stack([c_re.astype(_F32), -c_im.astype(_F32)]).reshape(2, nm, _GROUPS_PER_TILE, c, p)
    wc = jnp.einsum('rmgcp,gh->mrgphc', cc, eye)
    wc = wc.reshape(nm, 2 * _GROUPS_PER_TILE * p, _GROUPS_PER_TILE * c).astype(_BF16)
    lam = jnp.concatenate([lb_re.reshape(nm, _HALF), lb_im.reshape(nm, _HALF)], axis=-1)
    lam8 = jnp.broadcast_to(lam[:, None, :], (nm, _SUBLANES, _STATE_W))
    return lam8, wb, wc


def _s5_glu_layer(x2, bsz, seq, gain, d_skip, lam8, wb, wc, w_glu, *, tile=256):
    t_tok, d = x2.shape
    nm = lam8.shape[0]
    lc = tile // _SUBLANES
    nt = seq // tile
    n_steps = bsz * nt
    assert (d // (nm // 2)) % _LANES == 0
    r = jnp.arange(tile)
    col = (r % _SUBLANES) * lc + r // _SUBLANES
    perm = (col[:, None] == jnp.arange(tile)[None, :]).astype(_BF16)
    perm_t = perm.T
    const3 = lambda s: (0, 0, 0)
    const2 = lambda s: (0, 0)
    once = dict(pipeline_mode=pl.Buffered(1))
    return pl.pallas_call(
        functools.partial(_s5_kernel, lc=lc, tiles_per_seq=nt),
        grid=(n_steps + 1,),
        in_specs=[
            pl.BlockSpec((tile, d), lambda s: (jnp.minimum(s, n_steps - 1), 0)),
            pl.BlockSpec((1, d), const2),
            pl.BlockSpec((1, d), const2),
            pl.BlockSpec((nm, _SUBLANES, _STATE_W), const3, **once),
            pl.BlockSpec((nm, _LANES, _STATE_W), const3, **once),
            pl.BlockSpec((nm, _STATE_W, _LANES), const3, **once),
            pl.BlockSpec((tile, tile), const2),
            pl.BlockSpec((tile, tile), const2),
            pl.BlockSpec((d, 2 * d), const2, **once),
        ],
        out_specs=pl.BlockSpec((tile, d), lambda s: (jnp.maximum(s - 1, 0), 0)),
        out_shape=jax.ShapeDtypeStruct((t_tok, d), _F32),
        scratch_shapes=[
            pltpu.VMEM((tile, d), _F32),
            pltpu.VMEM((nm, tile, _LANES), _BF16),
            pltpu.VMEM((nm, tile, _LANES), _F32),
            pltpu.VMEM((tile, _STATE_W), _F32),
            pltpu.VMEM((tile, _STATE_W), _F32),
            pltpu.VMEM((tile, _STATE_W), _BF16),
            pltpu.VMEM((tile, _STATE_W), _BF16),
            pltpu.VMEM((nm, _SUBLANES, _STATE_W), _F32),
            pltpu.VMEM((nm, _SUBLANES, _STATE_W), _F32),
            pltpu.VMEM((nm, 3 * _SUBLANES, _STATE_W), _F32),
            pltpu.VMEM((tile, d), _BF16),
            pltpu.VMEM((tile, d), _F32),
        ],
        compiler_params=_params(("arbitrary",)),
        name="s5_glu_layer",
    )(x2, gain.reshape(1, d).astype(_F32), d_skip.reshape(1, d).astype(_F32), lam8, wb, wc, perm, perm_t, w_glu)


def _ffn_kernel(x_ref, g_ref, fg_ref, win_hbm, wout_hbm, o_ref, hn_ref, wg_buf, wu_buf, wo_buf, sem,
                *, layer, hidden, tf, final_norm, row_split):
    i = pl.program_id(0)
    n_i = pl.num_programs(0)
    nf = hidden // tf
    rows = o_ref.shape[0] // row_split
    slabs = [slice(s * rows, (s + 1) * rows) for s in range(row_split)]

    def copies(f, slot):
        return (
            pltpu.make_async_copy(win_hbm.at[layer, :, pl.ds(f * tf, tf)], wg_buf.at[slot], sem.at[0, slot]),
            pltpu.make_async_copy(win_hbm.at[layer, :, pl.ds(hidden + f * tf, tf)], wu_buf.at[slot],
                                  sem.at[1, slot]),
            pltpu.make_async_copy(wout_hbm.at[layer, pl.ds(f * tf, tf), :], wo_buf.at[slot], sem.at[2, slot]),
        )

    def start(f, slot):
        for cp in copies(f, slot):
            cp.start()

    def wait(f, slot):
        for cp in copies(f, slot):
            cp.wait()

    n_slots = wg_buf.shape[0]
    ahead = n_slots - 1
    total = n_i * nf

    @pl.when(i == 0)
    def _first_fetches():
        for g0 in range(ahead):
            start(g0 % nf, g0 % n_slots)

    for sl in slabs:
        x = x_ref[sl, :]
        hn_ref[sl, :] = (x * _rms_scale(x) * g_ref[...]).astype(_BF16)
        o_ref[sl, :] = x

    def f_body(f, c):
        g = i * nf + f
        slot = lax.rem(g, n_slots)
        wait(f, slot)

        @pl.when(g + ahead < total)
        def _prefetch():
            start(lax.rem(f + ahead, nf), lax.rem(g + ahead, n_slots))

        wg = wg_buf[slot].astype(_BF16)
        wu = wu_buf[slot].astype(_BF16)
        wo = wo_buf[slot].astype(_BF16)
        for sl in slabs:
            hn = hn_ref[sl, :]
            a = jnp.dot(hn, wg, preferred_element_type=_F32)
            u = jnp.dot(hn, wu, preferred_element_type=_F32)
            act = (a * _sigmoid(a) * u).astype(_BF16)
            o_ref[sl, :] += jnp.dot(act, wo, preferred_element_type=_F32)
        return c

    lax.fori_loop(0, nf, f_body, 0)

    if final_norm:
        for sl in slabs:
            y = o_ref[sl, :]
            o_ref[sl, :] = y * _rms_scale(y) * fg_ref[...]


def _ffn_residual(x2, gain, w_in, w_out, layer, final_gain=None, *, tm=1024, tf=256, row_split=2, n_slots=3):
    t_tok, d = x2.shape
    hidden = w_out.shape[1]
    final_norm = final_gain is not None
    fg = (final_gain if final_norm else gain).reshape(1, d).astype(_F32)
    return pl.pallas_call(
        functools.partial(_ffn_kernel, layer=layer, hidden=hidden, tf=tf, final_norm=final_norm,
                          row_split=row_split),
        grid=(t_tok // tm,),
        in_specs=[
            pl.BlockSpec((tm, d), lambda i: (i, 0)),
            pl.BlockSpec((1, d), lambda i: (0, 0)),
            pl.BlockSpec((1, d), lambda i: (0, 0)),
            pl.BlockSpec(memory_space=pl.ANY),
            pl.BlockSpec(memory_space=pl.ANY),
        ],
        out_specs=pl.BlockSpec((tm, d), lambda i: (i, 0)),
        out_shape=jax.ShapeDtypeStruct((t_tok, d), _F32),
        scratch_shapes=[
            pltpu.VMEM((tm, d), _BF16),
            pltpu.VMEM((n_slots, d, tf), w_in.dtype),
            pltpu.VMEM((n_slots, d, tf), w_in.dtype),
            pltpu.VMEM((n_slots, tf, d), w_out.dtype),
            pltpu.SemaphoreType.DMA((3, n_slots)),
        ],
        compiler_params=_params(("arbitrary",)),
        name="ffn_residual",
    )(x2, gain.reshape(1, d).astype(_F32), fg, w_in, w_out)


_PERM_ROWS = 256


def _perm_matrix(dil, rows=_PERM_ROWS):
    dst = jnp.arange(rows)
    src = (dst % (rows // dil)) * dil + dst // (rows // dil)
    return (src[:, None] == jnp.arange(rows)[None, :]).astype(_BF16)


def _to_residue_major(y, p_ref, dil, store):
    tm = y.shape[0]
    run = _PERM_ROWS // dil
    for j in range(tm // _PERM_ROWS):
        yp = jnp.dot(p_ref[...], y[j * _PERM_ROWS:(j + 1) * _PERM_ROWS], preferred_element_type=_F32).astype(_BF16)
        for r in range(dil):
            store(r, j * run, yp[r * run:(r + 1) * run])


def _qkv_kernel(*refs, dils, nj):
    n_groups = len(dils)
    n_p = sum(1 for dl in dils if dl > 1)
    x_ref, gq_ref, gkv_ref, wq_ref, wkv_ref = refs[:5]
    p_refs = refs[5:5 + n_p]
    q_refs = refs[5 + n_p:5 + n_p + n_groups]
    kv_refs = refs[5 + n_p + n_groups:5 + n_p + 2 * n_groups]
    hn_ref = refs[-1]
    tm = hn_ref.shape[1]
    p_of = {}
    for dl in dils:
        if dl > 1:
            p_of[dl] = p_refs[len(p_of)]
    j = pl.program_id(2)

    @pl.when(j == 0)
    def _start():
        for sb in range(tm // _PERM_ROWS):
            rows = slice(sb * _PERM_ROWS, (sb + 1) * _PERM_ROWS)
            x = x_ref[rows, :]
            xr = x * _rms_scale(x)
            hq = (xr * gq_ref[...]).astype(_BF16)
            hn_ref[n_groups, rows, :] = (xr * gkv_ref[...]).astype(_BF16)
            for g, dl in enumerate(dils):
                if dl == 1:
                    hn_ref[g, rows, :] = hq
                else:
                    run = _PERM_ROWS // dl
                    yp = jnp.dot(p_of[dl][...], hq, preferred_element_type=_F32).astype(_BF16)
                    for r in range(dl):
                        dst = r * (tm // dl) + sb * run
                        hn_ref[g, dst:dst + run, :] = yp[r * run:(r + 1) * run]

    for g, dl in enumerate(dils):
        @pl.when((j >= g * nj) & (j < (g + 1) * nj))
        def _q_tile(g=g, dl=dl):
            res = jnp.dot(hn_ref[g], wq_ref[...].astype(_BF16), preferred_element_type=_F32).astype(_BF16)
            for r in range(dl):
                q_refs[g][r] = res[r * (tm // dl):(r + 1) * (tm // dl)]

    @pl.when(j >= n_groups * nj)
    def _kv_tile():
        kv = jnp.dot(hn_ref[n_groups], wkv_ref[...], preferred_element_type=_F32).astype(_BF16)
        for g, dl in enumerate(dils):
            if dl == 1:
                kv_refs[g][0] = kv
            else:
                def store(r, row, block, g=g):
                    kv_refs[g][r, row:row + block.shape[0], :] = block
                _to_residue_major(kv, p_of[dl], dl, store)


def _qkv_project(x2, bsz, seq, q_gain, kv_gain, w_q, layer, w_kv, dils, *, tm=1024, tn=512):
    t_tok, d = x2.shape
    n_groups = len(dils)
    qw = w_q.shape[2] // n_groups
    kvw = w_kv.shape[1]
    assert qw % tn == 0 and kvw % tn == 0
    nj = qw // tn
    nkv = kvw // tn
    nt = seq // tm
    perms = [_perm_matrix(dl) for dl in dils if dl > 1]

    def q_spec(g, dl):
        return pl.BlockSpec((None, dl, tm // dl, tn),
                            lambda b, t, j: (b, 0, t, jnp.clip(j - g * nj, 0, nj - 1)))

    outs = pl.pallas_call(
        functools.partial(_qkv_kernel, dils=tuple(dils), nj=nj),
        grid=(bsz, nt, n_groups * nj + nkv),
        in_specs=([pl.BlockSpec((tm, d), lambda b, t, j: (b * nt + t, 0)),
                   pl.BlockSpec((1, d), lambda b, t, j: (0, 0)),
                   pl.BlockSpec((1, d), lambda b, t, j: (0, 0)),
                   pl.BlockSpec((None, d, tn), lambda b, t, j: (layer, 0, jnp.minimum(j, n_groups * nj - 1))),
                   pl.BlockSpec((d, tn), lambda b, t, j: (0, jnp.clip(j - n_groups * nj, 0, nkv - 1)))]
                  + [pl.BlockSpec((_PERM_ROWS, _PERM_ROWS), lambda b, t, j: (0, 0))] * len(perms)),
        out_specs=([q_spec(g, dl) for g, dl in enumerate(dils)]
                   + [pl.BlockSpec((None, dl, tm // dl, tn),
                                   lambda b, t, j: (b, 0, t, jnp.clip(j - n_groups * nj, 0, nkv - 1)))
                      for dl in dils]),
        out_shape=([jax.ShapeDtypeStruct((bsz, dl, seq // dl, qw), _BF16) for dl in dils]
                   + [jax.ShapeDtypeStruct((bsz, dl, seq // dl, kvw), _BF16) for dl in dils]),
        scratch_shapes=[pltpu.VMEM((n_groups + 1, tm, d), _BF16)],
        compiler_params=_params(("parallel", "parallel", "arbitrary")),
        name="qkv_project",
    )(x2, q_gain.reshape(1, d).astype(_F32), kv_gain.reshape(1, d).astype(_F32), w_q, w_kv, *perms)
    return outs[:n_groups], outs[n_groups:]


def _attn_kernel(q_ref, kp_ref, kc_ref, vp_ref, vc_ref, o_ref, st_ref, *, blk, rep, scale):
    i = pl.program_id(2)
    nq = q_ref.shape[0] // blk
    qi = lax.broadcasted_iota(jnp.int32, (blk, 2 * blk), 0)
    si = lax.broadcasted_iota(jnp.int32, (blk, 2 * blk), 1)
    dist = qi + blk - si
    band = (dist >= 0) & (dist <= blk)
    band_first = band & ((i > 0) | (si >= blk))
    lane = lax.broadcasted_iota(jnp.int32, (blk, _LANES), 1)
    n_kv = kc_ref.shape[1] // _HEAD_DIM
    for sub in range(nq):
        valid = jnp.concatenate([band_first if sub == 0 else band] * rep, axis=0)
        rows = slice(sub * blk, (sub + 1) * blk)
        st = jnp.zeros((blk, _LANES), _F32)
        for kvh in range(n_kv):
            ks = slice(kvh * _HEAD_DIM, (kvh + 1) * _HEAD_DIM)
            if sub == 0:
                k = jnp.concatenate([kp_ref[:, ks], kc_ref[0:blk, ks]], axis=0)
                v = jnp.concatenate([vp_ref[:, ks], vc_ref[0:blk, ks]], axis=0)
            else:
                k = kc_ref[(sub - 1) * blk:(sub + 1) * blk, ks]
                v = vc_ref[(sub - 1) * blk:(sub + 1) * blk, ks]
            q = jnp.concatenate(
                [q_ref[rows, (kvh * rep + rp) * _HEAD_DIM:(kvh * rep + rp + 1) * _HEAD_DIM] for rp in range(rep)],
                axis=0)
            s = lax.dot_general(q, k, (((1,), (1,)), ((), ())), preferred_element_type=_F32) * scale
            s = jnp.where(valid, s, _NEG_INF)
            m = jnp.max(s, axis=-1, keepdims=True)
            p = jnp.exp(s - m)
            l = jnp.sum(p, axis=-1, keepdims=True)
            pn = (p * (1.0 / l)).astype(_BF16)
            o = jnp.dot(pn, v, preferred_element_type=_F32)
            lse = m + jnp.log(l)
            for rp in range(rep):
                hd = kvh * rep + rp
                o_ref[rows, hd * _HEAD_DIM:(hd + 1) * _HEAD_DIM] = o[rp * blk:(rp + 1) * blk].astype(o_ref.dtype)
                st = jnp.where(lane == hd, lse[rp * blk:(rp + 1) * blk], st)
        st_ref[rows, :] = st


def _dilated_attention(q, kv, window, dilation, *, max_nq=4):
    bsz, dil, n, qw = q.shape
    assert dil == dilation
    kw = kv.shape[3] // 2
    n_heads = qw // _HEAD_DIM
    rep = n_heads // (kw // _HEAD_DIM)
    blk = window // dilation
    nq = min(max_nq, n // blk)
    assert n % (nq * blk) == 0 and blk == _LANES and n_heads <= _LANES
    nb = n // (nq * blk)
    prev = lambda i: jnp.maximum(nq * i - 1, 0)
    return pl.pallas_call(
        functools.partial(_attn_kernel, blk=blk, rep=rep, scale=_HEAD_DIM ** -0.5),
        grid=(bsz, dilation, nb),
        in_specs=[
            pl.BlockSpec((None, None, nq * blk, qw), lambda b, r, i: (b, r, i, 0)),
            pl.BlockSpec((None, None, blk, kw), lambda b, r, i: (b, r, prev(i), 0)),
            pl.BlockSpec((None, None, nq * blk, kw), lambda b, r, i: (b, r, i, 0)),
            pl.BlockSpec((None, None, blk, kw), lambda b, r, i: (b, r, prev(i), 1)),
            pl.BlockSpec((None, None, nq * blk, kw), lambda b, r, i: (b, r, i, 1)),
        ],
        out_specs=[
            pl.BlockSpec((None, None, nq * blk, qw), lambda b, r, i: (b, r, i, 0)),
            pl.BlockSpec((None, None, nq * blk, _LANES), lambda b, r, i: (b, r, i, 0)),
        ],
        out_shape=[
            jax.ShapeDtypeStruct((bsz, dilation, n, qw), _BF16),
            jax.ShapeDtypeStruct((bsz, dilation, n, _LANES), _F32),
        ],
        compiler_params=_params(("parallel", "parallel", "arbitrary")),
        name=f"dilated_attention_d{dilation}",
    )(q, kv, kv, kv, kv)


def _combine_kernel(*refs, dils, n_heads):
    n_groups = len(dils)
    n_p = sum(1 for dl in dils if dl > 1)
    o_in = refs[:n_groups]
    s_in = refs[n_groups:2 * n_groups]
    x_ref, wo_ref = refs[2 * n_groups:2 * n_groups + 2]
    pt_refs = refs[2 * n_groups + 2:2 * n_groups + 2 + n_p]
    out_ref, onat_ref, snat_ref, comb_ref = refs[2 * n_groups + 2 + n_p:]
    tm = x_ref.shape[0]

    p_of = {}
    for dl in dils:
        if dl > 1:
            p_of[dl] = pt_refs[len(p_of)]

    for j in range(tm // _PERM_ROWS):
        rs = slice(j * _PERM_ROWS, (j + 1) * _PERM_ROWS)
        for g, dl in enumerate(dils):
            if dl == 1:
                onat_ref[g, rs, :] = o_in[g][0, rs, :]
                snat_ref[g, rs, :] = s_in[g][0, rs, :]
                continue
            run = _PERM_ROWS // dl
            pt = p_of[dl][...]
            oc = jnp.concatenate([o_in[g][r, j * run:(j + 1) * run, :] for r in range(dl)], axis=0)
            onat_ref[g, rs, :] = jnp.dot(pt, oc, preferred_element_type=_F32).astype(_BF16)
            sc = jnp.concatenate([s_in[g][r, j * run:(j + 1) * run, :] for r in range(dl)], axis=0)
            s_hi = sc.astype(_BF16)
            rem = sc - s_hi.astype(_F32)
            s_mid = rem.astype(_BF16)
            s_lo = (rem - s_mid.astype(_F32)).astype(_BF16)
            snat_ref[g, rs, :] = (jnp.dot(pt, s_hi, preferred_element_type=_F32)
                                  + jnp.dot(pt, s_mid, preferred_element_type=_F32)
                                  + jnp.dot(pt, s_lo, preferred_element_type=_F32))

    for j in range(tm // _PERM_ROWS):
        rs = slice(j * _PERM_ROWS, (j + 1) * _PERM_ROWS)
        lses = [snat_ref[g, rs, :] for g in range(n_groups)]
        mx = lses[0]
        for l in lses[1:]:
            mx = jnp.maximum(mx, l)
        es = [jnp.exp(l - mx) for l in lses]
        den = es[0]
        for e in es[1:]:
            den = den + e
        inv = 1.0 / den
        wts = [e * inv for e in es]
        for hd in range(n_heads):
            cs = slice(hd * _HEAD_DIM, (hd + 1) * _HEAD_DIM)
            base = onat_ref[0, rs, cs].astype(_F32)
            acc = base
            for g in range(1, n_groups):
                acc = acc + wts[g][:, hd:hd + 1] * (onat_ref[g, rs, cs].astype(_F32) - base)
            comb_ref[rs, cs] = acc.astype(_BF16)
        out_ref[rs, :] = x_ref[rs, :] + jnp.dot(comb_ref[rs, :], wo_ref[...], preferred_element_type=_F32)


def _combine_project(outs, stats, x2, w_o, layer, dils, *, tm=512):
    t_tok, d = x2.shape
    bsz, _, _, qw = outs[0].shape
    seq = t_tok // bsz
    nt = seq // tm
    n_groups = len(outs)
    perms_t = [_perm_matrix(dl).T for dl in dils if dl > 1]
    grp = lambda b, t: (b, 0, t, 0)
    return pl.pallas_call(
        functools.partial(_combine_kernel, dils=tuple(dils), n_heads=qw // _HEAD_DIM),
        grid=(bsz, nt),
        in_specs=([pl.BlockSpec((None, dl, tm // dl, qw), grp) for dl in dils]
                  + [pl.BlockSpec((None, dl, tm // dl, _LANES), grp) for dl in dils]
                  + [pl.BlockSpec((tm, d), lambda b, t: (b * nt + t, 0)),
                     pl.BlockSpec((None, qw, d), lambda b, t: (layer, 0, 0), pipeline_mode=pl.Buffered(1))]
                  + [pl.BlockSpec((_PERM_ROWS, _PERM_ROWS), lambda b, t: (0, 0))] * len(perms_t)),
        out_specs=pl.BlockSpec((tm, d), lambda b, t: (b * nt + t, 0)),
        out_shape=jax.ShapeDtypeStruct((t_tok, d), _F32),
        scratch_shapes=[pltpu.VMEM((n_groups, tm, qw), _BF16),
                        pltpu.VMEM((n_groups, tm, _LANES), _F32),
                        pltpu.VMEM((tm, qw), _BF16)],
        compiler_params=_params(("parallel", "arbitrary")),
        name="combine_project",
    )(*outs, *stats, x2, w_o, *perms_t)


def kernel(x, s5_lam_re, s5_lam_im, s5_log_dt, s5_b_re, s5_b_im, s5_c_re, s5_c_im, s5_d, s5_w_glu, a_norm_mix, ffn_norm, ffn_w_in, ffn_w_out, b_norm_mix, attn_w_q, attn_w_o, kv_norm, w_kv, final_norm):
    bsz, seq, d = x.shape
    n_a = a_norm_mix.shape[0]
    n_b = b_norm_mix.shape[0]
    depth = n_a + n_b
    n_groups = len(_DILATED_PATTERNS)
    dils = [dl for _, dl in _DILATED_PATTERNS]
    x2 = x.reshape(bsz * seq, d).astype(_F32)
    assert n_b == 1, "the fused q/kv projection covers the single attention layer of this model"
    for layer in range(depth):
        if layer < n_a:
            i = layer
            lam8, wb, wc = _s5_tables(s5_lam_re[i], s5_lam_im[i], s5_log_dt[i], s5_b_re[i], s5_b_im[i],
                                      s5_c_re[i], s5_c_im[i])
            x2 = _s5_glu_layer(x2, bsz, seq, a_norm_mix[i], s5_d[i], lam8, wb, wc, s5_w_glu[i].astype(_BF16))
        else:
            j = layer - n_a
            qs, kvs = _qkv_project(x2, bsz, seq, b_norm_mix[j], kv_norm, attn_w_q, j, w_kv.astype(_BF16), dils)
            outs, stats = [], []
            for g, (window, dilation) in enumerate(_DILATED_PATTERNS):
                o, st = _dilated_attention(qs[g], kvs[g], window, dilation)
                outs.append(o)
                stats.append(st)
            x2 = _combine_project(outs, stats, x2, attn_w_o.astype(_BF16), j, dils)
        final_gain = final_norm if layer == depth - 1 else None
        x2 = _ffn_residual(x2, ffn_norm[layer], ffn_w_in, ffn_w_out, layer, final_gain)
    return x2.reshape(bsz, seq, d).astype(x.dtype)
```
